```python
import math
import jax, jax.numpy as jnp
from jax import lax
import numpy as np

D_MODEL = 2048
BATCH = 4
SEQ = 4096
DEPTH = 2

MEM_LEN = 256
N_MIXERS = 2
HEAD_DIM = 128
MIX_WIDTH = D_MODEL
CROSS_HEADS = 4
CROSS_WIDTH = CROSS_HEADS * HEAD_DIM
TOK_WIDTH = MIX_WIDTH - CROSS_WIDTH
CONV_WIDTH = 3
DIFF_HEAD_DIM = 64
DIFF_HEADS = TOK_WIDTH // (2 * DIFF_HEAD_DIM)
IN_WIDTH = 3 * TOK_WIDTH + CROSS_WIDTH + MIX_WIDTH
ROPE_THETA = 10000.0
RMS_EPS = 1e-6
Q_BLOCK = 128
NEG_BIG = -1e30
N_CONV_LAYERS = (DEPTH + 1) // 2
N_DIFF_LAYERS = DEPTH // 2

kernel_name = "hybrid_shortconv_diffattn_memxattn"


def _rms_norm(x, g):
    xf = x.astype(jnp.float32)
    xf = xf * lax.rsqrt(jnp.mean(xf * xf, axis=-1, keepdims=True) + RMS_EPS)
    return (xf * g.astype(jnp.float32)).astype(x.dtype)


def _rope(t, positions):
    d = t.shape[-1]
    inv_freq = ROPE_THETA ** (-jnp.arange(0, d, 2, dtype=jnp.float32) / d)
    ang = positions.astype(jnp.float32)[:, :, None] * inv_freq
    cos = jnp.cos(ang)[:, :, None, None, :]
    sin = jnp.sin(ang)[:, :, None, None, :]
    tf = t.astype(jnp.float32)
    t1, t2 = tf[..., : d // 2], tf[..., d // 2:]
    out = jnp.concatenate([t1 * cos - t2 * sin, t2 * cos + t1 * sin], axis=-1)
    return out.astype(t.dtype)


def _short_conv_mixer(proj_tok, conv_w):
    x_in, gate_b, gate_c = jnp.split(proj_tok, 3, axis=-1)
    u = gate_c * x_in
    s = u.shape[1]
    up = jnp.pad(u, ((0, 0), (CONV_WIDTH - 1, 0), (0, 0)))
    w = conv_w.astype(u.dtype)
    conv = sum(up[:, k:k + s, :] * w[:, k] for k in range(CONV_WIDTH))
    return gate_b * conv


def _diff_attention(q, k, v, lam):
    b, s, h, _, d = q.shape
    nb = s // Q_BLOCK
    scale = d ** -0.5
    qb = q.reshape(b, nb, Q_BLOCK, h, 2, d).transpose(1, 0, 2, 3, 4, 5)
    key_idx = jnp.arange(s)

    def block(args):
        q_blk, blk = args
        q_idx = blk * Q_BLOCK + jnp.arange(Q_BLOCK)
        scores = jnp.einsum('bqhmd,bkhmd->bhmqk', q_blk, k,
                            preferred_element_type=jnp.float32) * scale
        causal = key_idx[None, :] <= q_idx[:, None]
        scores = jnp.where(causal, scores, NEG_BIG)
        probs = jax.nn.softmax(scores, axis=-1)
        weights = probs[:, :, 0] - lam * probs[:, :, 1]
        return jnp.einsum('bhqk,bkhe->bqhe', weights.astype(v.dtype), v)

    out = lax.map(block, (qb, jnp.arange(nb)))
    return out.transpose(1, 0, 2, 3, 4).reshape(b, s, h, v.shape[-1])


def _diff_attention_mixer(proj_tok, positions, lam_params, subln_g, layer_idx):
    b, s, _ = proj_tok.shape
    q, k, v = jnp.split(proj_tok, 3, axis=-1)
    q = _rope(q.reshape(b, s, DIFF_HEADS, 2, DIFF_HEAD_DIM), positions)
    k = _rope(k.reshape(b, s, DIFF_HEADS, 2, DIFF_HEAD_DIM), positions)
    v = v.reshape(b, s, DIFF_HEADS, 2 * DIFF_HEAD_DIM)
    lambda_init = 0.8 - 0.6 * math.exp(-0.3 * layer_idx)
    lp = lam_params.astype(jnp.float32)
    lam = jnp.exp(jnp.sum(lp[0] * lp[1])) - jnp.exp(jnp.sum(lp[2] * lp[3])) + lambda_init
    o = _diff_attention(q, k, v, lam)
    o = _rms_norm(o, subln_g) * (1.0 - lambda_init)
    return o.reshape(b, s, TOK_WIDTH)


def _memory_cross_attention(q_mem, mem_n, w_kv):
    b, s, _ = q_mem.shape
    q = q_mem.reshape(b, s, CROSS_HEADS, HEAD_DIM)
    kv = mem_n @ w_kv
    km, vm = jnp.split(kv, 2, axis=-1)
    km = km.reshape(b, -1, CROSS_HEADS, HEAD_DIM)
    vm = vm.reshape(b, -1, CROSS_HEADS, HEAD_DIM)
    scores = jnp.einsum('bshd,bmhd->bhsm', q, km,
                        preferred_element_type=jnp.float32) * (HEAD_DIM ** -0.5)
    probs = jax.nn.softmax(scores, axis=-1)
    o = jnp.einsum('bhsm,bmhd->bshd', probs.astype(vm.dtype), vm)
    return o.reshape(b, s, CROSS_WIDTH)


def setup_inputs(seed: int = 0) -> dict:
    key = jax.random.key(seed)
    ks = jax.random.split(key, 14)
    f32 = jnp.float32
    x = jax.random.normal(ks[0], (BATCH, SEQ, D_MODEL), f32)
    mem = jax.random.normal(ks[1], (BATCH, MEM_LEN, D_MODEL), f32)
    offsets = jax.random.randint(ks[2], (BATCH, 1), 0, 1024, dtype=jnp.int32)
    positions = offsets + jnp.arange(SEQ, dtype=jnp.int32)[None, :]
    pre_norm = 1.0 + 0.05 * jax.random.normal(ks[3], (DEPTH, D_MODEL), f32)
    post_norm = 1.0 + 0.05 * jax.random.normal(ks[4], (DEPTH, D_MODEL), f32)
    mem_norm = 1.0 + 0.05 * jax.random.normal(ks[5], (DEPTH, D_MODEL), f32)
    w_in = jax.random.normal(ks[6], (DEPTH, D_MODEL, IN_WIDTH), f32) * D_MODEL ** -0.5
    w_kv_mem = jax.random.normal(ks[7], (DEPTH, D_MODEL, 2 * CROSS_WIDTH), f32) * D_MODEL ** -0.5
    w_out = jax.random.normal(ks[8], (DEPTH, MIX_WIDTH, D_MODEL), f32) * MIX_WIDTH ** -0.5
    conv_w = jax.random.normal(ks[9], (N_CONV_LAYERS, TOK_WIDTH, CONV_WIDTH), f32) * CONV_WIDTH ** -0.5
    diff_lambda = 0.1 * jax.random.normal(ks[10], (N_DIFF_LAYERS, 4, DIFF_HEAD_DIM), f32)
    diff_subln = 1.0 + 0.05 * jax.random.normal(ks[11], (N_DIFF_LAYERS, 2 * DIFF_HEAD_DIM), f32)
    return {"x": x, "mem": mem, "positions": positions, "pre_norm": pre_norm,
            "post_norm": post_norm, "mem_norm": mem_norm, "w_in": w_in,
            "w_kv_mem": w_kv_mem, "w_out": w_out, "conv_w": conv_w,
            "diff_lambda": diff_lambda, "diff_subln": diff_subln}


def reference(x, mem, positions, pre_norm, post_norm, mem_norm, w_in, w_kv_mem,
              w_out, conv_w, diff_lambda, diff_subln):
    for i in range(DEPTH):
        h = _rms_norm(x, pre_norm[i])
        proj = h @ w_in[i]
        proj_tok = proj[..., : 3 * TOK_WIDTH]
        q_mem = proj[..., 3 * TOK_WIDTH: 3 * TOK_WIDTH + CROSS_WIDTH]
        gate = proj[..., 3 * TOK_WIDTH + CROSS_WIDTH:]
        if i % N_MIXERS == 0:
            tok_out = _short_conv_mixer(proj_tok, conv_w[i // N_MIXERS])
        else:
            tok_out = _diff_attention_mixer(proj_tok, positions, diff_lambda[i // N_MIXERS],
                                            diff_subln[i // N_MIXERS], i)
        mem_n = _rms_norm(mem, mem_norm[i])
        cross_out = _memory_cross_attention(q_mem, mem_n, w_kv_mem[i])
        branch = jnp.concatenate([tok_out, cross_out], axis=-1) * jax.nn.silu(gate)
        y = branch @ w_out[i]
        x = x + _rms_norm(y, post_norm[i])
    return x
```

```python
import functools
import math

import jax
import jax.numpy as jnp
from jax import lax
from jax.experimental import pallas as pl
from jax.experimental.pallas import tpu as pltpu

HEAD_DIM = 128
CROSS_HEADS = 4
CROSS_WIDTH = CROSS_HEADS * HEAD_DIM
CONV_WIDTH = 3
DIFF_HEAD_DIM = 64
ROPE_THETA = 10000.0
RMS_EPS = 1e-6
NEG_BIG = -1e30
N_MIXERS = 2

LANES = 128
VMEM_LIMIT_BYTES = 56 * 1024 * 1024

F32 = jnp.float32
BF16 = jnp.bfloat16


def _rms_scale(xf):
    return xf * lax.rsqrt(jnp.mean(xf * xf, axis=-1, keepdims=True) + RMS_EPS)


def _rope_table_kernel(pos_ref, inv_ref, sign_ref, cos_ref, sin_ref):
    ang = pos_ref[...].astype(F32) * inv_ref[...]
    cos_ref[...] = jnp.cos(ang)
    sin_ref[...] = jnp.sin(ang) * sign_ref[...]


def _rope_tables(positions, tm):
    m = positions.size
    half = DIFF_HEAD_DIM // 2
    inv_freq = ROPE_THETA ** (-jnp.arange(0, DIFF_HEAD_DIM, 2, dtype=F32) / DIFF_HEAD_DIM)
    reps = LANES // half
    inv = jnp.tile(inv_freq, reps).reshape(1, LANES)
    sign = jnp.tile(jnp.concatenate([-jnp.ones((half,), F32), jnp.ones((half,), F32)]),
                    LANES // DIFF_HEAD_DIM).reshape(1, LANES)
    pos_b = jnp.broadcast_to(positions.reshape(m, 1), (m, LANES))
    return pl.pallas_call(
        _rope_table_kernel,
        grid=(m // tm,),
        in_specs=[pl.BlockSpec((tm, LANES), lambda i: (i, 0)),
                  pl.BlockSpec((1, LANES), lambda i: (0, 0)),
                  pl.BlockSpec((1, LANES), lambda i: (0, 0))],
        out_specs=[pl.BlockSpec((tm, LANES), lambda i: (i, 0)),
                   pl.BlockSpec((tm, LANES), lambda i: (i, 0))],
        out_shape=[jax.ShapeDtypeStruct((m, LANES), F32),
                   jax.ShapeDtypeStruct((m, LANES), F32)],
        name="rope_tables",
    )(pos_b, inv, sign)


def _in_proj_kernel(*refs, rope_tiles, tn):
    if rope_tiles:
        x_ref, g_ref, w_ref, cos_ref, sin_ref, o_ref, h_ref = refs
    else:
        x_ref, g_ref, w_ref, o_ref, h_ref = refs
    j = pl.program_id(1)

    @pl.when(j == 0)
    def _():
        h_ref[...] = (_rms_scale(x_ref[...]) * g_ref[...]).astype(BF16)

    def project():
        return jnp.dot(h_ref[...], w_ref[...], preferred_element_type=F32)

    if not rope_tiles:
        o_ref[...] = project().astype(o_ref.dtype)
        return

    @pl.when(j < rope_tiles)
    def _():
        acc = project()
        cos = cos_ref[...]
        sin = sin_ref[...]
        lane = lax.broadcasted_iota(jnp.int32, cos.shape, 1)
        first_half = (lane % DIFF_HEAD_DIM) < (DIFF_HEAD_DIM // 2)
        half = DIFF_HEAD_DIM // 2
        for c in range(tn // LANES):
            t = acc[:, c * LANES:(c + 1) * LANES]
            partner = jnp.where(first_half, pltpu.roll(t, LANES - half, 1), pltpu.roll(t, half, 1))
            o_ref[:, c * LANES:(c + 1) * LANES] = (t * cos + partner * sin).astype(o_ref.dtype)

    @pl.when(j >= rope_tiles)
    def _():
        o_ref[...] = project().astype(o_ref.dtype)


def _in_proj(x2d, g, w_bf16, rope, *, tm, tn, rope_cols):
    m, d = x2d.shape
    n = w_bf16.shape[1]
    rope_tiles = 0 if rope is None else rope_cols // tn
    in_specs = [pl.BlockSpec((tm, d), lambda i, j: (i, 0)),
                pl.BlockSpec((1, d), lambda i, j: (0, 0)),
                pl.BlockSpec((d, tn), lambda i, j: (0, j))]
    args = [x2d, g.reshape(1, d), w_bf16]
    if rope is not None:
        assert rope_cols % tn == 0
        in_specs += [pl.BlockSpec((tm, LANES), lambda i, j: (i, 0)),
                     pl.BlockSpec((tm, LANES), lambda i, j: (i, 0))]
        args += list(rope)
    return pl.pallas_call(
        functools.partial(_in_proj_kernel, rope_tiles=rope_tiles, tn=tn),
        grid=(m // tm, n // tn),
        in_specs=in_specs,
        out_specs=pl.BlockSpec((tm, tn), lambda i, j: (i, j)),
        out_shape=jax.ShapeDtypeStruct((m, n), BF16),
        scratch_shapes=[pltpu.VMEM((tm, d), BF16)],
        compiler_params=pltpu.CompilerParams(
            dimension_semantics=("arbitrary", "arbitrary"),
            vmem_limit_bytes=VMEM_LIMIT_BYTES),
        name="in_proj_rope" if rope_tiles else "in_proj",
    )(*args)


def _mem_kv_kernel(mem_ref, g_ref, w_ref, o_ref):
    h = (_rms_scale(mem_ref[...]) * g_ref[...]).astype(BF16)
    o_ref[...] = jnp.dot(h, w_ref[...], preferred_element_type=F32).astype(o_ref.dtype)


def _mem_kv(mem2d, g, w_bf16, *, mem_len):
    m, d = mem2d.shape
    n = w_bf16.shape[1]
    return pl.pallas_call(
        _mem_kv_kernel,
        grid=(m // mem_len,),
        in_specs=[pl.BlockSpec((mem_len, d), lambda i: (i, 0)),
                  pl.BlockSpec((1, d), lambda i: (0, 0)),
                  pl.BlockSpec((d, n), lambda i: (0, 0))],
        out_specs=pl.BlockSpec((mem_len, n), lambda i: (i, 0)),
        out_shape=jax.ShapeDtypeStruct((m, n), BF16),
        compiler_params=pltpu.CompilerParams(
            dimension_semantics=("arbitrary",), vmem_limit_bytes=VMEM_LIMIT_BYTES),
        name="mem_kv",
    )(mem2d, g.reshape(1, d), w_bf16)


def _diff_attn_kernel(lam_ref, q_ref, k_ref, v_ref, g_ref, o_ref,
                      vt_ref, acc1_ref, acc2_ref, m1_ref, l1_ref, m2_ref, l2_ref,
                      *, t, lambda_init):
    qi = pl.program_id(2)
    n_kv = vt_ref.shape[0]

    @pl.when(qi == 0)
    def _():
        for j in range(n_kv):
            vt_ref[j] = v_ref[j * t:(j + 1) * t, :].astype(F32).T.astype(BF16)

    q_t = (q_ref[...] * (DIFF_HEAD_DIM ** -0.5)).astype(F32).T
    row = lax.broadcasted_iota(jnp.int32, q_t.shape, 0)
    zero = jnp.zeros_like(q_t)
    q_maps = (jnp.where(row < DIFF_HEAD_DIM, q_t, zero).astype(BF16),
              jnp.where(row >= DIFF_HEAD_DIM, q_t, zero).astype(BF16))
    stats = ((acc1_ref, m1_ref, l1_ref), (acc2_ref, m2_ref, l2_ref))

    for acc_ref, m_ref, l_ref in stats:
        acc_ref[...] = jnp.zeros_like(acc_ref)
        m_ref[...] = jnp.full_like(m_ref, NEG_BIG)
        l_ref[...] = jnp.zeros_like(l_ref)

    def step(j, masked):
        start = pl.multiple_of(j * t, t)
        k_tile = k_ref[pl.ds(start, t), :]
        v_t = vt_ref[j]
        for q_map, (acc_ref, m_ref, l_ref) in zip(q_maps, stats):
            s = jnp.dot(k_tile, q_map, preferred_element_type=F32)
            if masked:
                key = lax.broadcasted_iota(jnp.int32, s.shape, 0)
                qry = lax.broadcasted_iota(jnp.int32, s.shape, 1)
                s = jnp.where(key <= qry, s, NEG_BIG)
            m_old = m_ref[...]
            m_new = jnp.maximum(m_old, jnp.max(s, axis=0, keepdims=True))
            alpha = jnp.exp(m_old - m_new)
            p = jnp.exp(s - m_new)
            l_ref[...] = alpha * l_ref[...] + jnp.sum(p, axis=0, keepdims=True)
            acc_ref[...] = alpha * acc_ref[...] + jnp.dot(
                v_t, p.astype(BF16), preferred_element_type=F32)
            m_ref[...] = m_new

    def body(j, carry):
        step(j, masked=False)
        return carry

    lax.fori_loop(0, qi, body, 0)
    step(qi, masked=True)

    lp = lam_ref[...]
    lam = (jnp.exp(jnp.sum(lp[0:1] * lp[1:2], axis=1, keepdims=True))
           - jnp.exp(jnp.sum(lp[2:3] * lp[3:4], axis=1, keepdims=True)) + lambda_init)
    o = acc1_ref[...] / l1_ref[...] - lam * (acc2_ref[...] / l2_ref[...])
    o = o * lax.rsqrt(jnp.mean(o * o, axis=0, keepdims=True) + RMS_EPS)
    o_ref[...] = ((o.T * g_ref[...]) * (1.0 - lambda_init)).astype(o_ref.dtype)


def _diff_attention(proj, lam_params, subln_g, *, batch, seq, heads, layer_idx, t):
    m = proj.shape[0]
    nq = seq // t
    k_off = heads
    v_off = 2 * heads
    lambda_init = 0.8 - 0.6 * math.exp(-0.3 * layer_idx)
    return pl.pallas_call(
        functools.partial(_diff_attn_kernel, t=t, lambda_init=lambda_init),
        grid=(batch, heads, nq),
        in_specs=[pl.BlockSpec(lam_params.shape, lambda b, h, q: (0, 0)),
                  pl.BlockSpec((t, HEAD_DIM), lambda b, h, q: (b * nq + q, h)),
                  pl.BlockSpec((seq, HEAD_DIM), lambda b, h, q: (b, k_off + h)),
                  pl.BlockSpec((seq, HEAD_DIM), lambda b, h, q: (b, v_off + h)),
                  pl.BlockSpec((1, HEAD_DIM), lambda b, h, q: (0, 0))],
        out_specs=pl.BlockSpec((t, HEAD_DIM), lambda b, h, q: (b * nq + q, h)),
        out_shape=jax.ShapeDtypeStruct((m, heads * HEAD_DIM), BF16),
        scratch_shapes=[pltpu.VMEM((nq, HEAD_DIM, t), BF16),
                        pltpu.VMEM((HEAD_DIM, t), F32), pltpu.VMEM((HEAD_DIM, t), F32),
                        pltpu.VMEM((1, t), F32), pltpu.VMEM((1, t), F32),
                        pltpu.VMEM((1, t), F32), pltpu.VMEM((1, t), F32)],
        compiler_params=pltpu.CompilerParams(
            dimension_semantics=("arbitrary", "arbitrary", "arbitrary"),
            vmem_limit_bytes=VMEM_LIMIT_BYTES),
        name="diff_attention",
    )(lam_params, proj, proj, proj, subln_g.reshape(1, HEAD_DIM))


def _silu(g):
    return g * (1.0 / (1.0 + jnp.exp(-g)))


def _mix_out_kernel(*refs, conv, tm, tok_width, tiles_per_seq, mem_len):
    if conv:
        (xin_ref, gb_ref, gc_ref, cw_ref, qm_ref, gate_a_ref, gate_b_ref, kv_ref, wout_ref,
         x_ref, gpost_ref, o_ref, br_ref, u_ref) = refs
    else:
        (tok_ref, qm_ref, gate_a_ref, gate_b_ref, kv_ref, wout_ref,
         x_ref, gpost_ref, o_ref, br_ref) = refs
    half_gate = gate_a_ref.shape[1]

    def gate_cols(lo, hi):
        if hi <= half_gate:
            return _silu(gate_a_ref[:, lo:hi].astype(F32))
        assert lo >= half_gate
        return _silu(gate_b_ref[:, lo - half_gate:hi - half_gate].astype(F32))

    chunk = half_gate // 2
    if conv:
        pad = 8
        first = (pl.program_id(0) % tiles_per_seq) == 0

        @pl.when(first)
        def _():
            u_ref[0:pad, :] = jnp.zeros((pad, tok_width), F32)

        @pl.when(jnp.logical_not(first))
        def _():
            u_ref[0:pad, :] = u_ref[tm:tm + pad, :]

        u_ref[pad:pad + tm, :] = gc_ref[...].astype(F32) * xin_ref[...].astype(F32)
        for lo in range(0, tok_width, chunk):
            hi = lo + chunk
            w = cw_ref[:, lo:hi]
            conv_out = (u_ref[pad:pad + tm, lo:hi] * w[2:3]
                        + u_ref[pad - 1:pad - 1 + tm, lo:hi] * w[1:2]
                        + u_ref[pad - 2:pad - 2 + tm, lo:hi] * w[0:1])
            tok = gb_ref[:, lo:hi].astype(F32) * conv_out
            br_ref[:, lo:hi] = (tok * gate_cols(lo, hi)).astype(BF16)
    else:
        for lo in range(0, tok_width, chunk):
            hi = lo + chunk
            br_ref[:, lo:hi] = (tok_ref[:, lo:hi].astype(F32) * gate_cols(lo, hi)).astype(BF16)

    scale = HEAD_DIM ** -0.5
    for h in range(CROSS_HEADS):
        lo = h * HEAD_DIM
        q_h = qm_ref[:, lo:lo + HEAD_DIM]
        k_h = kv_ref[:, lo:lo + HEAD_DIM]
        v_h = kv_ref[:, CROSS_WIDTH + lo:CROSS_WIDTH + lo + HEAD_DIM]
        s = lax.dot_general(q_h, k_h, (((1,), (1,)), ((), ())),
                            preferred_element_type=F32) * scale
        p = jnp.exp(s - jnp.max(s, axis=-1, keepdims=True))
        o_h = jnp.dot(p.astype(BF16), v_h, preferred_element_type=F32)
        o_h = o_h / jnp.sum(p, axis=-1, keepdims=True)
        col = tok_width + lo
        br_ref[:, col:col + HEAD_DIM] = (o_h * gate_cols(col, col + HEAD_DIM)).astype(BF16)

    y = jnp.dot(br_ref[...], wout_ref[...], preferred_element_type=F32)
    o_ref[...] = x_ref[...] + _rms_scale(y) * gpost_ref[...]


def _mix_out(proj, tok, conv_w, kv, w_out_bf16, x2d, g_post, *, seq, mem_len, tm):
    m, d = x2d.shape
    mix_width = w_out_bf16.shape[0]
    tok_width = mix_width - CROSS_WIDTH
    conv = tok is None
    tiles_per_seq = seq // tm
    qm_blk = (3 * tok_width) // CROSS_WIDTH
    half_gate = mix_width // 2
    gate_blk = (3 * tok_width + CROSS_WIDTH) // half_gate
    assert qm_blk * CROSS_WIDTH == 3 * tok_width
    assert gate_blk * half_gate == 3 * tok_width + CROSS_WIDTH

    tail_specs = [pl.BlockSpec((tm, CROSS_WIDTH), lambda i: (i, qm_blk)),
                  pl.BlockSpec((tm, half_gate), lambda i: (i, gate_blk)),
                  pl.BlockSpec((tm, half_gate), lambda i: (i, gate_blk + 1)),
                  pl.BlockSpec((mem_len, 2 * CROSS_WIDTH), lambda i: (i // tiles_per_seq, 0)),
                  pl.BlockSpec((mix_width, d), lambda i: (0, 0)),
                  pl.BlockSpec((tm, d), lambda i: (i, 0)),
                  pl.BlockSpec((1, d), lambda i: (0, 0))]
    tail_args = [proj, proj, proj, kv, w_out_bf16, x2d, g_post.reshape(1, d)]
    scratch = [pltpu.VMEM((tm, mix_width), BF16)]
    if conv:
        head_specs = [pl.BlockSpec((tm, tok_width), lambda i: (i, 0)),
                      pl.BlockSpec((tm, tok_width), lambda i: (i, 1)),
                      pl.BlockSpec((tm, tok_width), lambda i: (i, 2)),
                      pl.BlockSpec((CONV_WIDTH, tok_width), lambda i: (0, 0))]
        head_args = [proj, proj, proj, conv_w.T]
        scratch.append(pltpu.VMEM((tm + 8, tok_width), F32))
    else:
        head_specs = [pl.BlockSpec((tm, tok_width), lambda i: (i, 0))]
        head_args = [tok]
    return pl.pallas_call(
        functools.partial(_mix_out_kernel, conv=conv, tm=tm, tok_width=tok_width,
                          tiles_per_seq=tiles_per_seq, mem_len=mem_len),
        grid=(m // tm,),
        in_specs=head_specs + tail_specs,
        out_specs=pl.BlockSpec((tm, d), lambda i: (i, 0)),
        out_shape=jax.ShapeDtypeStruct((m, d), F32),
        scratch_shapes=scratch,
        compiler_params=pltpu.CompilerParams(
            dimension_semantics=("arbitrary",), vmem_limit_bytes=VMEM_LIMIT_BYTES),
        name="mix_out_conv" if conv else "mix_out_attn",
    )(*head_args, *tail_args)


def kernel(x, mem, positions, pre_norm, post_norm, mem_norm, w_in, w_kv_mem, w_out,
           conv_w, diff_lambda, diff_subln):
    batch, seq, d = x.shape
    mem_len = mem.shape[1]
    depth = w_in.shape[0]
    tok_width = w_out.shape[1] - CROSS_WIDTH
    heads = tok_width // HEAD_DIM
    x2d = x.reshape(batch * seq, d)
    mem2d = mem.reshape(batch * mem_len, d)
    rope = _rope_tables(positions, tm=1024) if depth > 1 else None

    for i in range(depth):
        attn_layer = (i % N_MIXERS) == 1
        proj = _in_proj(x2d, pre_norm[i], w_in[i].astype(BF16), rope if attn_layer else None,
                        tm=1024, tn=1024, rope_cols=2 * tok_width)
        kv = _mem_kv(mem2d, mem_norm[i], w_kv_mem[i].astype(BF16), mem_len=mem_len)
        if attn_layer:
            tok = _diff_attention(proj, diff_lambda[i // N_MIXERS], diff_subln[i // N_MIXERS],
                                  batch=batch, seq=seq, heads=heads, layer_idx=i, t=512)
            conv = None
        else:
            tok = None
            conv = conv_w[i // N_MIXERS]
        x2d = _mix_out(proj, tok, conv, kv, w_out[i].astype(BF16), x2d, post_norm[i],
                       seq=seq, mem_len=mem_len, tm=256)
    return x2d.reshape(batch, seq, d)
```

```python
import functools
import math

import jax
import jax.numpy as jnp
from jax import lax
from jax.experimental import pallas as pl
from jax.experimental.pallas import tpu as pltpu

HEAD_DIM = 128
CROSS_HEADS = 4
CROSS_WIDTH = CROSS_HEADS * HEAD_DIM
CONV_WIDTH = 3
DIFF_HEAD_DIM = 64
ROPE_THETA = 10000.0
RMS_EPS = 1e-6
NEG_BIG = -1e30
N_MIXERS = 2
LOG2_E = math.log2(math.e)

LANES = 128
VMEM_LIMIT_BYTES = 56 * 1024 * 1024

F32 = jnp.float32
BF16 = jnp.bfloat16


def _rms_scale(xf):
    return xf * lax.rsqrt(jnp.mean(xf * xf, axis=-1, keepdims=True) + RMS_EPS)


def _rope_table_kernel(pos_ref, inv_ref, sign_ref, cos_ref, sin_ref):
    ang = pos_ref[...].astype(F32) * inv_ref[...]
    cos_ref[...] = jnp.cos(ang)
    sin_ref[...] = jnp.sin(ang) * sign_ref[...]


def _rope_tables(positions, tm):
    m = positions.size
    half = DIFF_HEAD_DIM // 2
    inv_freq = ROPE_THETA ** (-jnp.arange(0, DIFF_HEAD_DIM, 2, dtype=F32) / DIFF_HEAD_DIM)
    reps = LANES // half
    inv = jnp.tile(inv_freq, reps).reshape(1, LANES)
    sign = jnp.tile(jnp.concatenate([-jnp.ones((half,), F32), jnp.ones((half,), F32)]),
                    LANES // DIFF_HEAD_DIM).reshape(1, LANES)
    pos_b = jnp.broadcast_to(positions.reshape(m, 1), (m, LANES))
    return pl.pallas_call(
        _rope_table_kernel,
        grid=(m // tm,),
        in_specs=[pl.BlockSpec((tm, LANES), lambda i: (i, 0)),
                  pl.BlockSpec((1, LANES), lambda i: (0, 0)),
                  pl.BlockSpec((1, LANES), lambda i: (0, 0))],
        out_specs=[pl.BlockSpec((tm, LANES), lambda i: (i, 0)),
                   pl.BlockSpec((tm, LANES), lambda i: (i, 0))],
        out_shape=[jax.ShapeDtypeStruct((m, LANES), F32),
                   jax.ShapeDtypeStruct((m, LANES), F32)],
        name="rope_tables",
    )(pos_b, inv, sign)


def _in_proj_kernel(*refs, rope_tiles, tn):
    if rope_tiles:
        x_ref, g_ref, w_ref, cos_ref, sin_ref, o_ref, h_ref = refs
    else:
        x_ref, g_ref, w_ref, o_ref, h_ref = refs
    j = pl.program_id(1)

    @pl.when(j == 0)
    def _():
        h_ref[...] = (_rms_scale(x_ref[...]) * g_ref[...]).astype(BF16)

    def project():
        return jnp.dot(h_ref[...], w_ref[...], preferred_element_type=F32)

    if not rope_tiles:
        o_ref[...] = project().astype(o_ref.dtype)
        return

    @pl.when(j < rope_tiles)
    def _():
        acc = project()
        cos = cos_ref[...]
        sin = sin_ref[...]
        lane = lax.broadcasted_iota(jnp.int32, cos.shape, 1)
        first_half = (lane % DIFF_HEAD_DIM) < (DIFF_HEAD_DIM // 2)
        half = DIFF_HEAD_DIM // 2
        for c in range(tn // LANES):
            t = acc[:, c * LANES:(c + 1) * LANES]
            partner = jnp.where(first_half, pltpu.roll(t, LANES - half, 1), pltpu.roll(t, half, 1))
            o_ref[:, c * LANES:(c + 1) * LANES] = (t * cos + partner * sin).astype(o_ref.dtype)

    @pl.when(j >= rope_tiles)
    def _():
        o_ref[...] = project().astype(o_ref.dtype)


def _in_proj(x2d, g, w_bf16, rope, *, tm, tn, rope_cols):
    m, d = x2d.shape
    n = w_bf16.shape[1]
    rope_tiles = 0 if rope is None else rope_cols // tn
    in_specs = [pl.BlockSpec((tm, d), lambda i, j: (i, 0)),
                pl.BlockSpec((1, d), lambda i, j: (0, 0)),
                pl.BlockSpec((d, tn), lambda i, j: (0, j))]
    args = [x2d, g.reshape(1, d), w_bf16]
    if rope is not None:
        assert rope_cols % tn == 0
        in_specs += [pl.BlockSpec((tm, LANES), lambda i, j: (i, 0)),
                     pl.BlockSpec((tm, LANES), lambda i, j: (i, 0))]
        args += list(rope)
    return pl.pallas_call(
        functools.partial(_in_proj_kernel, rope_tiles=rope_tiles, tn=tn),
        grid=(m // tm, n // tn),
        in_specs=in_specs,
        out_specs=pl.BlockSpec((tm, tn), lambda i, j: (i, j)),
        out_shape=jax.ShapeDtypeStruct((m, n), BF16),
        scratch_shapes=[pltpu.VMEM((tm, d), BF16)],
        compiler_params=pltpu.CompilerParams(
            dimension_semantics=("arbitrary", "arbitrary"),
            vmem_limit_bytes=VMEM_LIMIT_BYTES),
        name="in_proj_rope" if rope_tiles else "in_proj",
    )(*args)


def _mem_kv_kernel(mem_ref, g_ref, w_ref, o_ref):
    h = (_rms_scale(mem_ref[...]) * g_ref[...]).astype(BF16)
    o_ref[...] = jnp.dot(h, w_ref[...], preferred_element_type=F32).astype(o_ref.dtype)


def _mem_kv(mem2d, g, w_bf16, *, mem_len):
    m, d = mem2d.shape
    n = w_bf16.shape[1]
    return pl.pallas_call(
        _mem_kv_kernel,
        grid=(m // mem_len,),
        in_specs=[pl.BlockSpec((mem_len, d), lambda i: (i, 0)),
                  pl.BlockSpec((1, d), lambda i: (0, 0)),
                  pl.BlockSpec((d, n), lambda i: (0, 0))],
        out_specs=pl.BlockSpec((mem_len, n), lambda i: (i, 0)),
        out_shape=jax.ShapeDtypeStruct((m, n), BF16),
        compiler_params=pltpu.CompilerParams(
            dimension_semantics=("arbitrary",), vmem_limit_bytes=VMEM_LIMIT_BYTES),
        name="mem_kv",
    )(mem2d, g.reshape(1, d), w_bf16)


def _diff_attn_kernel(lam_ref, q_ref, k_ref, v_ref, g_ref, o_ref,
                      vt_ref, sa_ref, sb_ref, acc1_ref, acc2_ref, m1_ref, l1_ref, m2_ref, l2_ref,
                      *, t, lambda_init):
    qi = pl.program_id(2)
    n_kv = vt_ref.shape[0]

    @pl.when(qi == 0)
    def _():
        for j in range(n_kv):
            vt_ref[j] = v_ref[j * t:(j + 1) * t, :].astype(F32).T.astype(BF16)

    q_t = (q_ref[...].astype(F32) * (LOG2_E * DIFF_HEAD_DIM ** -0.5)).T
    row = lax.broadcasted_iota(jnp.int32, q_t.shape, 0)
    zero = jnp.zeros_like(q_t)
    q_maps = (jnp.where(row < DIFF_HEAD_DIM, q_t, zero).astype(BF16),
              jnp.where(row >= DIFF_HEAD_DIM, q_t, zero).astype(BF16))
    stats = ((acc1_ref, m1_ref, l1_ref), (acc2_ref, m2_ref, l2_ref))

    for acc_ref, m_ref, l_ref in stats:
        acc_ref[...] = jnp.zeros_like(acc_ref)
        m_ref[...] = jnp.full_like(m_ref, NEG_BIG)
        l_ref[...] = jnp.zeros_like(l_ref)

    def scores(j, s_ref):
        start = pl.multiple_of(j * t, t)
        k_tile = k_ref[pl.ds(start, t), :]
        for mi, q_map in enumerate(q_maps):
            s_ref[mi] = jnp.dot(k_tile, q_map, preferred_element_type=F32)

    def consume(j, s_ref, masked):
        v_t = vt_ref[j]
        for mi, (acc_ref, m_ref, l_ref) in enumerate(stats):
            s = s_ref[mi]
            if masked:
                key = lax.broadcasted_iota(jnp.int32, s.shape, 0)
                qry = lax.broadcasted_iota(jnp.int32, s.shape, 1)
                s = jnp.where(key <= qry, s, NEG_BIG)
            m_old = m_ref[...]
            m_new = jnp.maximum(m_old, jnp.max(s, axis=0, keepdims=True))
            alpha = jnp.exp2(m_old - m_new)
            p = jnp.exp2(s - m_new)
            l_ref[...] = alpha * l_ref[...] + jnp.sum(p, axis=0, keepdims=True)
            acc_ref[...] = alpha * acc_ref[...] + jnp.dot(
                v_t, p.astype(BF16), preferred_element_type=F32)
            m_ref[...] = m_new

    scores(0, sa_ref)

    def pair(i, carry):
        j = 2 * i
        scores(j + 1, sb_ref)
        consume(j, sa_ref, masked=False)
        scores(j + 2, sa_ref)
        consume(j + 1, sb_ref, masked=False)
        return carry

    lax.fori_loop(0, qi // 2, pair, 0)

    @pl.when(qi % 2 == 0)
    def _():
        consume(qi, sa_ref, masked=True)

    @pl.when(qi % 2 == 1)
    def _():
        scores(qi, sb_ref)
        consume(qi - 1, sa_ref, masked=False)
        consume(qi, sb_ref, masked=True)

    lp = lam_ref[...]
    lam = (jnp.exp(jnp.sum(lp[0:1] * lp[1:2], axis=1, keepdims=True))
           - jnp.exp(jnp.sum(lp[2:3] * lp[3:4], axis=1, keepdims=True)) + lambda_init)
    o = acc1_ref[...] / l1_ref[...] - lam * (acc2_ref[...] / l2_ref[...])
    o = o * lax.rsqrt(jnp.mean(o * o, axis=0, keepdims=True) + RMS_EPS)
    o_ref[...] = ((o.T * g_ref[...]) * (1.0 - lambda_init)).astype(o_ref.dtype)


def _diff_attention(proj, lam_params, subln_g, *, batch, seq, heads, layer_idx, t):
    m = proj.shape[0]
    nq = seq // t
    k_off = heads
    v_off = 2 * heads
    lambda_init = 0.8 - 0.6 * math.exp(-0.3 * layer_idx)
    return pl.pallas_call(
        functools.partial(_diff_attn_kernel, t=t, lambda_init=lambda_init),
        grid=(batch, heads, nq),
        in_specs=[pl.BlockSpec(lam_params.shape, lambda b, h, q: (0, 0)),
                  pl.BlockSpec((t, HEAD_DIM), lambda b, h, q: (b * nq + q, h)),
                  pl.BlockSpec((seq, HEAD_DIM), lambda b, h, q: (b, k_off + h)),
                  pl.BlockSpec((seq, HEAD_DIM), lambda b, h, q: (b, v_off + h)),
                  pl.BlockSpec((1, HEAD_DIM), lambda b, h, q: (0, 0))],
        out_specs=pl.BlockSpec((t, HEAD_DIM), lambda b, h, q: (b * nq + q, h)),
        out_shape=jax.ShapeDtypeStruct((m, heads * HEAD_DIM), BF16),
        scratch_shapes=[pltpu.VMEM((nq, HEAD_DIM, t), BF16),
                        pltpu.VMEM((2, t, t), F32), pltpu.VMEM((2, t, t), F32),
                        pltpu.VMEM((HEAD_DIM, t), F32), pltpu.VMEM((HEAD_DIM, t), F32),
                        pltpu.VMEM((1, t), F32), pltpu.VMEM((1, t), F32),
                        pltpu.VMEM((1, t), F32), pltpu.VMEM((1, t), F32)],
        compiler_params=pltpu.CompilerParams(
            dimension_semantics=("arbitrary", "arbitrary", "arbitrary"),
            vmem_limit_bytes=VMEM_LIMIT_BYTES),
        name="diff_attention",
    )(lam_params, proj, proj, proj, subln_g.reshape(1, HEAD_DIM))


def _silu(g):
    return g * (1.0 / (1.0 + jnp.exp(-g)))


def _mix_out_kernel(*refs, conv, tm, tok_width, tiles_per_seq, mem_len):
    if conv:
        (xin_ref, gb_ref, gc_ref, cw_ref, qm_ref, gate_a_ref, gate_b_ref, kv_ref, wout_ref,
         x_ref, gpost_ref, o_ref, br_ref, u_ref) = refs
    else:
        (tok_ref, qm_ref, gate_a_ref, gate_b_ref, kv_ref, wout_ref,
         x_ref, gpost_ref, o_ref, br_ref) = refs
    half_gate = gate_a_ref.shape[1]

    def gate_cols(lo, hi):
        if hi <= half_gate:
            return _silu(gate_a_ref[:, lo:hi].astype(F32))
        assert lo >= half_gate
        return _silu(gate_b_ref[:, lo - half_gate:hi - half_gate].astype(F32))

    chunk = half_gate // 2
    if conv:
        pad = 8
        first = (pl.program_id(0) % tiles_per_seq) == 0

        @pl.when(first)
        def _():
            u_ref[0:pad, :] = jnp.zeros((pad, tok_width), F32)

        @pl.when(jnp.logical_not(first))
        def _():
            u_ref[0:pad, :] = u_ref[tm:tm + pad, :]

        u_ref[pad:pad + tm, :] = gc_ref[...].astype(F32) * xin_ref[...].astype(F32)
        for lo in range(0, tok_width, chunk):
            hi = lo + chunk
            w = cw_ref[:, lo:hi]
            conv_out = (u_ref[pad:pad + tm, lo:hi] * w[2:3]
                        + u_ref[pad - 1:pad - 1 + tm, lo:hi] * w[1:2]
                        + u_ref[pad - 2:pad - 2 + tm, lo:hi] * w[0:1])
            tok = gb_ref[:, lo:hi].astype(F32) * conv_out
            br_ref[:, lo:hi] = (tok * gate_cols(lo, hi)).astype(BF16)
    else:
        for lo in range(0, tok_width, chunk):
            hi = lo + chunk
            br_ref[:, lo:hi] = (tok_ref[:, lo:hi].astype(F32) * gate_cols(lo, hi)).astype(BF16)

    scale = HEAD_DIM ** -0.5
    for h in range(CROSS_HEADS):
        lo = h * HEAD_DIM
        q_h = qm_ref[:, lo:lo + HEAD_DIM]
        k_h = kv_ref[:, lo:lo + HEAD_DIM]
        v_h = kv_ref[:, CROSS_WIDTH + lo:CROSS_WIDTH + lo + HEAD_DIM]
        s = lax.dot_general(q_h, k_h, (((1,), (1,)), ((), ())),
                            preferred_element_type=F32) * scale
        p = jnp.exp(s - jnp.max(s, axis=-1, keepdims=True))
        o_h = jnp.dot(p.astype(BF16), v_h, preferred_element_type=F32)
        o_h = o_h / jnp.sum(p, axis=-1, keepdims=True)
        col = tok_width + lo
        br_ref[:, col:col + HEAD_DIM] = (o_h * gate_cols(col, col + HEAD_DIM)).astype(BF16)

    y = jnp.dot(br_ref[...], wout_ref[...], preferred_element_type=F32)
    o_ref[...] = x_ref[...] + _rms_scale(y) * gpost_ref[...]


def _mix_out(proj, tok, conv_w, kv, w_out_bf16, x2d, g_post, *, seq, mem_len, tm):
    m, d = x2d.shape
    mix_width = w_out_bf16.shape[0]
    tok_width = mix_width - CROSS_WIDTH
    conv = tok is None
    tiles_per_seq = seq // tm
    qm_blk = (3 * tok_width) // CROSS_WIDTH
    half_gate = mix_width // 2
    gate_blk = (3 * tok_width + CROSS_WIDTH) // half_gate
    assert qm_blk * CROSS_WIDTH == 3 * tok_width
    assert gate_blk * half_gate == 3 * tok_width + CROSS_WIDTH

    tail_specs = [pl.BlockSpec((tm, CROSS_WIDTH), lambda i: (i, qm_blk)),
                  pl.BlockSpec((tm, half_gate), lambda i: (i, gate_blk)),
                  pl.BlockSpec((tm, half_gate), lambda i: (i, gate_blk + 1)),
                  pl.BlockSpec((mem_len, 2 * CROSS_WIDTH), lambda i: (i // tiles_per_seq, 0)),
                  pl.BlockSpec((mix_width, d), lambda i: (0, 0)),
                  pl.BlockSpec((tm, d), lambda i: (i, 0)),
                  pl.BlockSpec((1, d), lambda i: (0, 0))]
    tail_args = [proj, proj, proj, kv, w_out_bf16, x2d, g_post.reshape(1, d)]
    scratch = [pltpu.VMEM((tm, mix_width), BF16)]
    if conv:
        head_specs = [pl.BlockSpec((tm, tok_width), lambda i: (i, 0)),
                      pl.BlockSpec((tm, tok_width), lambda i: (i, 1)),
                      pl.BlockSpec((tm, tok_width), lambda i: (i, 2)),
                      pl.BlockSpec((CONV_WIDTH, tok_width), lambda i: (0, 0))]
        head_args = [proj, proj, proj, conv_w.T]
        scratch.append(pltpu.VMEM((tm + 8, tok_width), F32))
    else:
        head_specs = [pl.BlockSpec((tm, tok_width), lambda i: (i, 0))]
        head_args = [tok]
    return pl.pallas_call(
        functools.partial(_mix_out_kernel, conv=conv, tm=tm, tok_width=tok_width,
                          tiles_per_seq=tiles_per_seq, mem_len=mem_len),
        grid=(m // tm,),
        in_specs=head_specs + tail_specs,
        out_specs=pl.BlockSpec((tm, d), lambda i: (i, 0)),
        out_shape=jax.ShapeDtypeStruct((m, d), F32),
        scratch_shapes=scratch,
        compiler_params=pltpu.CompilerParams(
            dimension_semantics=("arbitrary",), vmem_limit_bytes=VMEM_LIMIT_BYTES),
        name="mix_out_conv" if conv else "mix_out_attn",
    )(*head_args, *tail_args)


def kernel(x, mem, positions, pre_norm, post_norm, mem_norm, w_in, w_kv_mem, w_out,
           conv_w, diff_lambda, diff_subln):
    batch, seq, d = x.shape
    mem_len = mem.shape[1]
    depth = w_in.shape[0]
    tok_width = w_out.shape[1] - CROSS_WIDTH
    heads = tok_width // HEAD_DIM
    x2d = x.reshape(batch * seq, d)
    mem2d = mem.reshape(batch * mem_len, d)
    rope = _rope_tables(positions, tm=1024) if depth > 1 else None

    for i in range(depth):
        attn_layer = (i % N_MIXERS) == 1
        proj = _in_proj(x2d, pre_norm[i], w_in[i].astype(BF16), rope if attn_layer else None,
                        tm=1024, tn=1024, rope_cols=2 * tok_width)
        kv = _mem_kv(mem2d, mem_norm[i], w_kv_mem[i].astype(BF16), mem_len=mem_len)
        if attn_layer:
            tok = _diff_attention(proj, diff_lambda[i // N_MIXERS], diff_subln[i // N_MIXERS],
                                  batch=batch, seq=seq, heads=heads, layer_idx=i, t=512)
            conv = None
        else:
            tok = None
            conv = conv_w[i // N_MIXERS]
        x2d = _mix_out(proj, tok, conv, kv, w_out[i].astype(BF16), x2d, post_norm[i],
                       seq=seq, mem_len=mem_len, tm=256)
    return x2d.reshape(batch, seq, d)
```

```python
import functools
import math

import jax
import jax.numpy as jnp
from jax import lax
from jax.experimental import pallas as pl
from jax.experimental.pallas import tpu as pltpu

HEAD_DIM = 128
CROSS_HEADS = 4
CROSS_WIDTH = CROSS_HEADS * HEAD_DIM
CONV_WIDTH = 3
DIFF_HEAD_DIM = 64
ROPE_THETA = 10000.0
RMS_EPS = 1e-6
NEG_BIG = -1e30
N_MIXERS = 2
LOG2_E = math.log2(math.e)

LANES = 128
BF16_SUBLANES = 16
CONV_PAD = 8
VMEM_LIMIT_BYTES = 56 * 1024 * 1024

F32 = jnp.float32
BF16 = jnp.bfloat16


def _rms_scale(xf):
    return xf * lax.rsqrt(jnp.mean(xf * xf, axis=-1, keepdims=True) + RMS_EPS)


def _rope_table_kernel(pos_ref, inv_ref, sign_ref, cos_ref, sin_ref):
    ang = pos_ref[...].astype(F32) * inv_ref[...]
    cos_ref[...] = jnp.cos(ang)
    sin_ref[...] = jnp.sin(ang) * sign_ref[...]


def _rope_tables(positions, tm):
    m = positions.size
    half = DIFF_HEAD_DIM // 2
    inv_freq = ROPE_THETA ** (-jnp.arange(0, DIFF_HEAD_DIM, 2, dtype=F32) / DIFF_HEAD_DIM)
    reps = LANES // half
    inv = jnp.tile(inv_freq, reps).reshape(1, LANES)
    sign = jnp.tile(jnp.concatenate([-jnp.ones((half,), F32), jnp.ones((half,), F32)]),
                    LANES // DIFF_HEAD_DIM).reshape(1, LANES)
    pos_b = jnp.broadcast_to(positions.reshape(m, 1), (m, LANES))
    return pl.pallas_call(
        _rope_table_kernel,
        grid=(m // tm,),
        in_specs=[pl.BlockSpec((tm, LANES), lambda i: (i, 0)),
                  pl.BlockSpec((1, LANES), lambda i: (0, 0)),
                  pl.BlockSpec((1, LANES), lambda i: (0, 0))],
        out_specs=[pl.BlockSpec((tm, LANES), lambda i: (i, 0)),
                   pl.BlockSpec((tm, LANES), lambda i: (i, 0))],
        out_shape=[jax.ShapeDtypeStruct((m, LANES), F32),
                   jax.ShapeDtypeStruct((m, LANES), F32)],
        name="rope_tables",
    )(pos_b, inv, sign)


def _in_proj_kernel(*refs, rope_tiles, tn):
    if rope_tiles:
        x_ref, g_ref, w_ref, cos_ref, sin_ref, o_ref, h_ref = refs
    else:
        x_ref, g_ref, w_ref, o_ref, h_ref = refs
    j = pl.program_id(1)

    @pl.when(j == 0)
    def _():
        h_ref[...] = (_rms_scale(x_ref[...]) * g_ref[...]).astype(BF16)

    def project():
        return jnp.dot(h_ref[...], w_ref[...], preferred_element_type=F32)

    if not rope_tiles:
        o_ref[...] = project().astype(o_ref.dtype)
        return

    @pl.when(j < rope_tiles)
    def _():
        acc = project()
        cos = cos_ref[...]
        sin = sin_ref[...]
        lane = lax.broadcasted_iota(jnp.int32, cos.shape, 1)
        first_half = (lane % DIFF_HEAD_DIM) < (DIFF_HEAD_DIM // 2)
        half = DIFF_HEAD_DIM // 2
        for c in range(tn // LANES):
            t = acc[:, c * LANES:(c + 1) * LANES]
            partner = jnp.where(first_half, pltpu.roll(t, LANES - half, 1), pltpu.roll(t, half, 1))
            o_ref[:, c * LANES:(c + 1) * LANES] = (t * cos + partner * sin).astype(o_ref.dtype)

    @pl.when(j >= rope_tiles)
    def _():
        o_ref[...] = project().astype(o_ref.dtype)


def _in_proj(x2d, g, w_bf16, rope, *, tm, tn, rope_cols):
    m, d = x2d.shape
    n = w_bf16.shape[1]
    rope_tiles = 0 if rope is None else rope_cols // tn
    in_specs = [pl.BlockSpec((tm, d), lambda i, j: (i, 0)),
                pl.BlockSpec((1, d), lambda i, j: (0, 0)),
                pl.BlockSpec((d, tn), lambda i, j: (0, j))]
    args = [x2d, g.reshape(1, d), w_bf16]
    if rope is not None:
        assert rope_cols % tn == 0
        in_specs += [pl.BlockSpec((tm, LANES), lambda i, j: (i, 0)),
                     pl.BlockSpec((tm, LANES), lambda i, j: (i, 0))]
        args += list(rope)
    return pl.pallas_call(
        functools.partial(_in_proj_kernel, rope_tiles=rope_tiles, tn=tn),
        grid=(m // tm, n // tn),
        in_specs=in_specs,
        out_specs=pl.BlockSpec((tm, tn), lambda i, j: (i, j)),
        out_shape=jax.ShapeDtypeStruct((m, n), BF16),
        scratch_shapes=[pltpu.VMEM((tm, d), BF16)],
        compiler_params=pltpu.CompilerParams(
            dimension_semantics=("arbitrary", "arbitrary"),
            vmem_limit_bytes=VMEM_LIMIT_BYTES),
        name="in_proj_rope" if rope_tiles else "in_proj",
    )(*args)


def _mem_kv_kernel(mem_ref, g_ref, w_ref, o_ref):
    h = (_rms_scale(mem_ref[...]) * g_ref[...]).astype(BF16)
    o_ref[...] = jnp.dot(h, w_ref[...], preferred_element_type=F32).astype(o_ref.dtype)


def _mem_kv(mem2d, g, w_bf16, *, mem_len):
    m, d = mem2d.shape
    n = w_bf16.shape[1]
    return pl.pallas_call(
        _mem_kv_kernel,
        grid=(m // mem_len,),
        in_specs=[pl.BlockSpec((mem_len, d), lambda i: (i, 0)),
                  pl.BlockSpec((1, d), lambda i: (0, 0)),
                  pl.BlockSpec((d, n), lambda i: (0, 0))],
        out_specs=pl.BlockSpec((mem_len, n), lambda i: (i, 0)),
        out_shape=jax.ShapeDtypeStruct((m, n), BF16),
        compiler_params=pltpu.CompilerParams(
            dimension_semantics=("arbitrary",), vmem_limit_bytes=VMEM_LIMIT_BYTES),
        name="mem_kv",
    )(mem2d, g.reshape(1, d), w_bf16)


def _diff_attn_kernel(lam_ref, q_ref, k_ref, v_ref, g_ref, o_ref,
                      vt_ref, sa_ref, sb_ref, mxa_ref, mxb_ref, acc1_ref, acc2_ref, m_ref,
                      *, t, lambda_init):
    qi = pl.program_id(2)
    n_kv = vt_ref.shape[0]
    n_maps = 2
    acc_refs = (acc1_ref, acc2_ref)
    buf_a = (sa_ref, mxa_ref)
    buf_b = (sb_ref, mxb_ref)

    @pl.when(qi == 0)
    def _():
        ones = jnp.ones((BF16_SUBLANES, t), BF16)
        for j in range(n_kv):
            vt_ref[j, 0:HEAD_DIM, :] = v_ref[j * t:(j + 1) * t, :].astype(F32).T.astype(BF16)
            vt_ref[j, HEAD_DIM:, :] = ones

    q_t = (q_ref[...].astype(F32) * (LOG2_E * DIFF_HEAD_DIM ** -0.5)).T
    row = lax.broadcasted_iota(jnp.int32, q_t.shape, 0)
    zero = jnp.zeros_like(q_t)
    q_maps = (jnp.where(row < DIFF_HEAD_DIM, q_t, zero).astype(BF16),
              jnp.where(row >= DIFF_HEAD_DIM, q_t, zero).astype(BF16))

    for mi in range(n_maps):
        acc_refs[mi][...] = jnp.zeros_like(acc_refs[mi])
        m_ref[mi] = jnp.full((1, t), NEG_BIG, F32)

    def scores(j, buf):
        s_ref, mx_ref = buf
        start = pl.multiple_of(j * t, t)
        k_tile = k_ref[pl.ds(start, t), :]
        for mi in range(n_maps):
            s = jnp.dot(k_tile, q_maps[mi], preferred_element_type=F32)
            s_ref[mi] = s
            mx_ref[mi] = jnp.max(s, axis=0, keepdims=True)

    def consume(j, buf, masked):
        s_ref, mx_ref = buf
        v_t = vt_ref[j]
        for mi in range(n_maps):
            s = s_ref[mi]
            if masked:
                key = lax.broadcasted_iota(jnp.int32, s.shape, 0)
                qry = lax.broadcasted_iota(jnp.int32, s.shape, 1)
                s = jnp.where(key <= qry, s, NEG_BIG)
                tile_max = jnp.max(s, axis=0, keepdims=True)
            else:
                tile_max = mx_ref[mi]
            m_old = m_ref[mi]
            m_new = jnp.maximum(m_old, tile_max)
            alpha = jnp.exp2(m_old - m_new)
            p = jnp.exp2((s - m_new).astype(BF16))
            acc_refs[mi][...] = alpha * acc_refs[mi][...] + jnp.dot(
                v_t, p, preferred_element_type=F32)
            m_ref[mi] = m_new

    scores(0, buf_a)

    def pair(i, carry):
        j = 2 * i
        scores(j + 1, buf_b)
        consume(j, buf_a, masked=False)
        scores(j + 2, buf_a)
        consume(j + 1, buf_b, masked=False)
        return carry

    lax.fori_loop(0, qi // 2, pair, 0)

    @pl.when(qi % 2 == 0)
    def _():
        consume(qi, buf_a, masked=True)

    @pl.when(qi % 2 == 1)
    def _():
        scores(qi, buf_b)
        consume(qi - 1, buf_a, masked=False)
        consume(qi, buf_b, masked=True)

    lp = lam_ref[...]
    lam = (jnp.exp(jnp.sum(lp[0:1] * lp[1:2], axis=1, keepdims=True))
           - jnp.exp(jnp.sum(lp[2:3] * lp[3:4], axis=1, keepdims=True)) + lambda_init)
    o1 = acc1_ref[0:HEAD_DIM, :] / acc1_ref[HEAD_DIM:HEAD_DIM + 1, :]
    o2 = acc2_ref[0:HEAD_DIM, :] / acc2_ref[HEAD_DIM:HEAD_DIM + 1, :]
    o = o1 - lam * o2
    o = o * lax.rsqrt(jnp.mean(o * o, axis=0, keepdims=True) + RMS_EPS)
    o_ref[...] = ((o.T * g_ref[...]) * (1.0 - lambda_init)).astype(o_ref.dtype)


def _diff_attention(proj, lam_params, subln_g, *, batch, seq, heads, layer_idx, t):
    m = proj.shape[0]
    nq = seq // t
    k_off = heads
    v_off = 2 * heads
    lambda_init = 0.8 - 0.6 * math.exp(-0.3 * layer_idx)
    acc_rows = HEAD_DIM + BF16_SUBLANES
    return pl.pallas_call(
        functools.partial(_diff_attn_kernel, t=t, lambda_init=lambda_init),
        grid=(batch, heads, nq),
        in_specs=[pl.BlockSpec(lam_params.shape, lambda b, h, q: (0, 0)),
                  pl.BlockSpec((t, HEAD_DIM), lambda b, h, q: (b * nq + q, h)),
                  pl.BlockSpec((seq, HEAD_DIM), lambda b, h, q: (b, k_off + h)),
                  pl.BlockSpec((seq, HEAD_DIM), lambda b, h, q: (b, v_off + h)),
                  pl.BlockSpec((1, HEAD_DIM), lambda b, h, q: (0, 0))],
        out_specs=pl.BlockSpec((t, HEAD_DIM), lambda b, h, q: (b * nq + q, h)),
        out_shape=jax.ShapeDtypeStruct((m, heads * HEAD_DIM), BF16),
        scratch_shapes=[pltpu.VMEM((nq, acc_rows, t), BF16),
                        pltpu.VMEM((2, t, t), F32), pltpu.VMEM((2, t, t), F32),
                        pltpu.VMEM((2, 1, t), F32), pltpu.VMEM((2, 1, t), F32),
                        pltpu.VMEM((acc_rows, t), F32), pltpu.VMEM((acc_rows, t), F32),
                        pltpu.VMEM((2, 1, t), F32)],
        compiler_params=pltpu.CompilerParams(
            dimension_semantics=("arbitrary", "arbitrary", "arbitrary"),
            vmem_limit_bytes=VMEM_LIMIT_BYTES),
        name="diff_attention",
    )(lam_params, proj, proj, proj, subln_g.reshape(1, HEAD_DIM))


def _silu(g):
    return g * (1.0 / (1.0 + jnp.exp(-g)))


def _mix_out_kernel(*refs, conv, tm, tok_width, tiles_per_seq, mem_len):
    if conv:
        (xin_ref, gb_ref, gc_ref, cw_ref, qm_ref, gate_a_ref, gate_b_ref, kv_ref, wout_ref,
         x_ref, gpost_ref, o_ref, br_ref, u_ref) = refs
    else:
        (tok_ref, qm_ref, gate_a_ref, gate_b_ref, kv_ref, wout_ref,
         x_ref, gpost_ref, o_ref, br_ref) = refs
    half_gate = gate_a_ref.shape[1]

    def gate_cols(lo, hi):
        if hi <= half_gate:
            return _silu(gate_a_ref[:, lo:hi].astype(F32))
        assert lo >= half_gate
        return _silu(gate_b_ref[:, lo - half_gate:hi - half_gate].astype(F32))

    chunk = half_gate // 2
    if conv:
        pad = CONV_PAD
        first = (pl.program_id(0) % tiles_per_seq) == 0

        @pl.when(first)
        def _():
            u_ref[0:pad, :] = jnp.zeros((pad, tok_width), F32)

        @pl.when(jnp.logical_not(first))
        def _():
            u_ref[0:pad, :] = u_ref[tm:tm + pad, :]

        u_ref[pad:pad + tm, :] = gc_ref[...].astype(F32) * xin_ref[...].astype(F32)
        for lo in range(0, tok_width, chunk):
            hi = lo + chunk
            w = cw_ref[:, lo:hi]
            conv_out = (u_ref[pad:pad + tm, lo:hi] * w[2:3]
                        + u_ref[pad - 1:pad - 1 + tm, lo:hi] * w[1:2]
                        + u_ref[pad - 2:pad - 2 + tm, lo:hi] * w[0:1])
            tok = gb_ref[:, lo:hi].astype(F32) * conv_out
            br_ref[:, lo:hi] = (tok * gate_cols(lo, hi)).astype(BF16)
    else:
        for lo in range(0, tok_width, chunk):
            hi = lo + chunk
            br_ref[:, lo:hi] = (tok_ref[:, lo:hi].astype(F32) * gate_cols(lo, hi)).astype(BF16)

    scale = HEAD_DIM ** -0.5
    for h in range(CROSS_HEADS):
        lo = h * HEAD_DIM
        q_h = qm_ref[:, lo:lo + HEAD_DIM]
        k_h = kv_ref[:, lo:lo + HEAD_DIM]
        v_h = kv_ref[:, CROSS_WIDTH + lo:CROSS_WIDTH + lo + HEAD_DIM]
        s = lax.dot_general(q_h, k_h, (((1,), (1,)), ((), ())),
                            preferred_element_type=F32) * scale
        p = jnp.exp(s - jnp.max(s, axis=-1, keepdims=True))
        o_h = jnp.dot(p.astype(BF16), v_h, preferred_element_type=F32)
        o_h = o_h / jnp.sum(p, axis=-1, keepdims=True)
        col = tok_width + lo
        br_ref[:, col:col + HEAD_DIM] = (o_h * gate_cols(col, col + HEAD_DIM)).astype(BF16)

    y = jnp.dot(br_ref[...], wout_ref[...], preferred_element_type=F32)
    o_ref[...] = x_ref[...] + _rms_scale(y) * gpost_ref[...]


def _mix_out(proj, tok, conv_w, kv, w_out_bf16, x2d, g_post, *, seq, mem_len, tm):
    m, d = x2d.shape
    mix_width = w_out_bf16.shape[0]
    tok_width = mix_width - CROSS_WIDTH
    conv = tok is None
    tiles_per_seq = seq // tm
    qm_blk = (3 * tok_width) // CROSS_WIDTH
    half_gate = mix_width // 2
    gate_blk = (3 * tok_width + CROSS_WIDTH) // half_gate
    assert qm_blk * CROSS_WIDTH == 3 * tok_width
    assert gate_blk * half_gate == 3 * tok_width + CROSS_WIDTH

    tail_specs = [pl.BlockSpec((tm, CROSS_WIDTH), lambda i: (i, qm_blk)),
                  pl.BlockSpec((tm, half_gate), lambda i: (i, gate_blk)),
                  pl.BlockSpec((tm, half_gate), lambda i: (i, gate_blk + 1)),
                  pl.BlockSpec((mem_len, 2 * CROSS_WIDTH), lambda i: (i // tiles_per_seq, 0)),
                  pl.BlockSpec((mix_width, d), lambda i: (0, 0)),
                  pl.BlockSpec((tm, d), lambda i: (i, 0)),
                  pl.BlockSpec((1, d), lambda i: (0, 0))]
    tail_args = [proj, proj, proj, kv, w_out_bf16, x2d, g_post.reshape(1, d)]
    scratch = [pltpu.VMEM((tm, mix_width), BF16)]
    if conv:
        head_specs = [pl.BlockSpec((tm, tok_width), lambda i: (i, 0)),
                      pl.BlockSpec((tm, tok_width), lambda i: (i, 1)),
                      pl.BlockSpec((tm, tok_width), lambda i: (i, 2)),
                      pl.BlockSpec((CONV_WIDTH, tok_width), lambda i: (0, 0))]
        head_args = [proj, proj, proj, conv_w.T]
        scratch.append(pltpu.VMEM((tm + CONV_PAD, tok_width), F32))
    else:
        head_specs = [pl.BlockSpec((tm, tok_width), lambda i: (i, 0))]
        head_args = [tok]
    return pl.pallas_call(
        functools.partial(_mix_out_kernel, conv=conv, tm=tm, tok_width=tok_width,
                          tiles_per_seq=tiles_per_seq, mem_len=mem_len),
        grid=(m // tm,),
        in_specs=head_specs + tail_specs,
        out_specs=pl.BlockSpec((tm, d), lambda i: (i, 0)),
        out_shape=jax.ShapeDtypeStruct((m, d), F32),
        scratch_shapes=scratch,
        compiler_params=pltpu.CompilerParams(
            dimension_semantics=("arbitrary",), vmem_limit_bytes=VMEM_LIMIT_BYTES),
        name="mix_out_conv" if conv else "mix_out_attn",
    )(*head_args, *tail_args)


def kernel(x, mem, positions, pre_norm, post_norm, mem_norm, w_in, w_kv_mem, w_out,
           conv_w, diff_lambda, diff_subln):
    batch, seq, d = x.shape
    mem_len = mem.shape[1]
    depth = w_in.shape[0]
    tok_width = w_out.shape[1] - CROSS_WIDTH
    heads = tok_width // HEAD_DIM
    x2d = x.reshape(batch * seq, d)
    mem2d = mem.reshape(batch * mem_len, d)
    rope = _rope_tables(positions, tm=1024) if depth > 1 else None

    for i in range(depth):
        attn_layer = (i % N_MIXERS) == 1
        proj = _in_proj(x2d, pre_norm[i], w_in[i].astype(BF16), rope if attn_layer else None,
                        tm=1024, tn=1024, rope_cols=2 * tok_width)
        kv = _mem_kv(mem2d, mem_norm[i], w_kv_mem[i].astype(BF16), mem_len=mem_len)
        if attn_layer:
            tok = _diff_attention(proj, diff_lambda[i // N_MIXERS], diff_subln[i // N_MIXERS],
                                  batch=batch, seq=seq, heads=heads, layer_idx=i, t=512)
            conv = None
        else:
            tok = None
            conv = conv_w[i // N_MIXERS]
        x2d = _mix_out(proj, tok, conv, kv, w_out[i].astype(BF16), x2d, post_norm[i],
                       seq=seq, mem_len=mem_len, tm=256)
    return x2d.reshape(batch, seq, d)
```

```python
import functools
import math

import jax
import jax.numpy as jnp
from jax import lax
from jax.experimental import pallas as pl
from jax.experimental.pallas import tpu as pltpu

HEAD_DIM = 128
CROSS_HEADS = 4
CROSS_WIDTH = CROSS_HEADS * HEAD_DIM
CONV_WIDTH = 3
DIFF_HEAD_DIM = 64
ROPE_THETA = 10000.0
RMS_EPS = 1e-6
NEG_BIG = -1e30
N_MIXERS = 2
LOG2_E = math.log2(math.e)

LANES = 128
BF16_SUBLANES = 16
CONV_PAD = 8
VMEM_LIMIT_BYTES = 56 * 1024 * 1024

F32 = jnp.float32
BF16 = jnp.bfloat16


def _rms_scale(xf):
    return xf * lax.rsqrt(jnp.mean(xf * xf, axis=-1, keepdims=True) + RMS_EPS)


def _rope_table_kernel(pos_ref, inv_ref, sign_ref, cos_ref, sin_ref):
    ang = pos_ref[...].astype(F32) * inv_ref[...]
    cos_ref[...] = jnp.cos(ang)
    sin_ref[...] = jnp.sin(ang) * sign_ref[...]


def _rope_tables(positions, tm):
    m = positions.size
    half = DIFF_HEAD_DIM // 2
    inv_freq = ROPE_THETA ** (-jnp.arange(0, DIFF_HEAD_DIM, 2, dtype=F32) / DIFF_HEAD_DIM)
    reps = LANES // half
    inv = jnp.tile(inv_freq, reps).reshape(1, LANES)
    sign = jnp.tile(jnp.concatenate([-jnp.ones((half,), F32), jnp.ones((half,), F32)]),
                    LANES // DIFF_HEAD_DIM).reshape(1, LANES)
    pos_b = jnp.broadcast_to(positions.reshape(m, 1), (m, LANES))
    return pl.pallas_call(
        _rope_table_kernel,
        grid=(m // tm,),
        in_specs=[pl.BlockSpec((tm, LANES), lambda i: (i, 0)),
                  pl.BlockSpec((1, LANES), lambda i: (0, 0)),
                  pl.BlockSpec((1, LANES), lambda i: (0, 0))],
        out_specs=[pl.BlockSpec((tm, LANES), lambda i: (i, 0)),
                   pl.BlockSpec((tm, LANES), lambda i: (i, 0))],
        out_shape=[jax.ShapeDtypeStruct((m, LANES), F32),
                   jax.ShapeDtypeStruct((m, LANES), F32)],
        name="rope_tables",
    )(pos_b, inv, sign)


def _in_proj_kernel(*refs, rope_tiles, tn):
    if rope_tiles:
        x_ref, g_ref, w_ref, cos_ref, sin_ref, o_ref, h_ref = refs
    else:
        x_ref, g_ref, w_ref, o_ref, h_ref = refs
    j = pl.program_id(1)

    @pl.when(j == 0)
    def _():
        h_ref[...] = (_rms_scale(x_ref[...]) * g_ref[...]).astype(BF16)

    def project():
        return jnp.dot(h_ref[...], w_ref[...], preferred_element_type=F32)

    if not rope_tiles:
        o_ref[...] = project().astype(o_ref.dtype)
        return

    @pl.when(j < rope_tiles)
    def _():
        acc = project()
        cos = cos_ref[...]
        sin = sin_ref[...]
        lane = lax.broadcasted_iota(jnp.int32, cos.shape, 1)
        first_half = (lane % DIFF_HEAD_DIM) < (DIFF_HEAD_DIM // 2)
        half = DIFF_HEAD_DIM // 2
        for c in range(tn // LANES):
            t = acc[:, c * LANES:(c + 1) * LANES]
            partner = jnp.where(first_half, pltpu.roll(t, LANES - half, 1), pltpu.roll(t, half, 1))
            o_ref[:, c * LANES:(c + 1) * LANES] = (t * cos + partner * sin).astype(o_ref.dtype)

    @pl.when(j >= rope_tiles)
    def _():
        o_ref[...] = project().astype(o_ref.dtype)


def _in_proj(x2d, g, w_bf16, rope, *, tm, tn, rope_cols):
    m, d = x2d.shape
    n = w_bf16.shape[1]
    rope_tiles = 0 if rope is None else rope_cols // tn
    in_specs = [pl.BlockSpec((tm, d), lambda i, j: (i, 0)),
                pl.BlockSpec((1, d), lambda i, j: (0, 0)),
                pl.BlockSpec((d, tn), lambda i, j: (0, j))]
    args = [x2d, g.reshape(1, d), w_bf16]
    if rope is not None:
        assert rope_cols % tn == 0
        in_specs += [pl.BlockSpec((tm, LANES), lambda i, j: (i, 0)),
                     pl.BlockSpec((tm, LANES), lambda i, j: (i, 0))]
        args += list(rope)
    return pl.pallas_call(
        functools.partial(_in_proj_kernel, rope_tiles=rope_tiles, tn=tn),
        grid=(m // tm, n // tn),
        in_specs=in_specs,
        out_specs=pl.BlockSpec((tm, tn), lambda i, j: (i, j)),
        out_shape=jax.ShapeDtypeStruct((m, n), BF16),
        scratch_shapes=[pltpu.VMEM((tm, d), BF16)],
        compiler_params=pltpu.CompilerParams(
            dimension_semantics=("arbitrary", "arbitrary"),
            vmem_limit_bytes=VMEM_LIMIT_BYTES),
        name="in_proj_rope" if rope_tiles else "in_proj",
    )(*args)


def _mem_kv_kernel(mem_ref, g_ref, w_ref, o_ref):
    h = (_rms_scale(mem_ref[...]) * g_ref[...]).astype(BF16)
    o_ref[...] = jnp.dot(h, w_ref[...], preferred_element_type=F32).astype(o_ref.dtype)


def _mem_kv(mem2d, g, w_bf16, *, mem_len):
    m, d = mem2d.shape
    n = w_bf16.shape[1]
    return pl.pallas_call(
        _mem_kv_kernel,
        grid=(m // mem_len,),
        in_specs=[pl.BlockSpec((mem_len, d), lambda i: (i, 0)),
                  pl.BlockSpec((1, d), lambda i: (0, 0)),
                  pl.BlockSpec((d, n), lambda i: (0, 0))],
        out_specs=pl.BlockSpec((mem_len, n), lambda i: (i, 0)),
        out_shape=jax.ShapeDtypeStruct((m, n), BF16),
        compiler_params=pltpu.CompilerParams(
            dimension_semantics=("arbitrary",), vmem_limit_bytes=VMEM_LIMIT_BYTES),
        name="mem_kv",
    )(mem2d, g.reshape(1, d), w_bf16)


def _diff_attn_kernel(lam_ref, q_ref, k_ref, v_ref, g_ref, o_ref,
                      vt_ref, sa_ref, sb_ref, mxa_ref, mxb_ref, acc1_ref, acc2_ref,
                      *, t, lambda_init):
    nq = vt_ref.shape[0]
    n_maps = 2
    acc_refs = (acc1_ref, acc2_ref)
    bufs = ((sa_ref, mxa_ref), (sb_ref, mxb_ref))

    ones = jnp.ones((BF16_SUBLANES, t), BF16)
    for j in range(nq):
        vt_ref[j, 0:HEAD_DIM, :] = v_ref[j * t:(j + 1) * t, :].astype(F32).T.astype(BF16)
        vt_ref[j, HEAD_DIM:, :] = ones

    lp = lam_ref[...]
    lam = (jnp.exp(jnp.sum(lp[0:1] * lp[1:2], axis=1, keepdims=True))
           - jnp.exp(jnp.sum(lp[2:3] * lp[3:4], axis=1, keepdims=True)) + lambda_init)
    key = lax.broadcasted_iota(jnp.int32, (t, t), 0)
    qry = lax.broadcasted_iota(jnp.int32, (t, t), 1)
    causal = key <= qry

    def q_maps_of(qi):
        q_t = (q_ref[qi * t:(qi + 1) * t, :].astype(F32) * (LOG2_E * DIFF_HEAD_DIM ** -0.5)).T
        row = lax.broadcasted_iota(jnp.int32, q_t.shape, 0)
        zero = jnp.zeros_like(q_t)
        return (jnp.where(row < DIFF_HEAD_DIM, q_t, zero).astype(BF16),
                jnp.where(row >= DIFF_HEAD_DIM, q_t, zero).astype(BF16))

    def scores(q_maps, j, diagonal, buf):
        s_ref, mx_ref = buf
        k_tile = k_ref[j * t:(j + 1) * t, :]
        for mi in range(n_maps):
            s = jnp.dot(k_tile, q_maps[mi], preferred_element_type=F32)
            if diagonal:
                s = jnp.where(causal, s, NEG_BIG)
            s_ref[mi] = s
            mx_ref[mi] = jnp.max(s, axis=0, keepdims=True)

    def consume(j, buf, m_old):
        s_ref, mx_ref = buf
        v_t = vt_ref[j]
        m_out = []
        for mi in range(n_maps):
            if m_old is None:
                m_new = mx_ref[mi]
                p = jnp.exp2(s_ref[mi] - m_new).astype(BF16)
                acc_refs[mi][...] = jnp.dot(v_t, p, preferred_element_type=F32)
            else:
                m_new = jnp.maximum(m_old[mi], mx_ref[mi])
                alpha = jnp.exp2(m_old[mi] - m_new)
                p = jnp.exp2(s_ref[mi] - m_new).astype(BF16)
                acc_refs[mi][...] = alpha * acc_refs[mi][...] + jnp.dot(
                    v_t, p, preferred_element_type=F32)
            m_out.append(m_new)
        return m_out

    def finalize(qi):
        o1 = acc1_ref[0:HEAD_DIM, :] / acc1_ref[HEAD_DIM:HEAD_DIM + 1, :]
        o2 = acc2_ref[0:HEAD_DIM, :] / acc2_ref[HEAD_DIM:HEAD_DIM + 1, :]
        o = o1 - lam * o2
        o = o * lax.rsqrt(jnp.mean(o * o, axis=0, keepdims=True) + RMS_EPS)
        o_ref[qi * t:(qi + 1) * t, :] = (
            (o.T * g_ref[...]) * (1.0 - lambda_init)).astype(o_ref.dtype)

    tiles = [(qi, j) for qi in range(nq) for j in range(qi + 1)]
    q_maps = q_maps_of(0)
    scores(q_maps, 0, True, bufs[0])
    m_run = None
    for n, (qi, j) in enumerate(tiles):
        if n + 1 < len(tiles):
            qn, jn = tiles[n + 1]
            if qn != qi:
                q_maps = q_maps_of(qn)
            scores(q_maps, jn, jn == qn, bufs[(n + 1) % 2])
        m_run = consume(j, bufs[n % 2], m_run)
        if j == qi:
            finalize(qi)
            m_run = None


def _diff_attention(proj, lam_params, subln_g, *, batch, seq, heads, layer_idx, t):
    m = proj.shape[0]
    nq = seq // t
    k_off = heads
    v_off = 2 * heads
    lambda_init = 0.8 - 0.6 * math.exp(-0.3 * layer_idx)
    acc_rows = HEAD_DIM + BF16_SUBLANES
    return pl.pallas_call(
        functools.partial(_diff_attn_kernel, t=t, lambda_init=lambda_init),
        grid=(batch, heads),
        in_specs=[pl.BlockSpec(lam_params.shape, lambda b, h: (0, 0)),
                  pl.BlockSpec((seq, HEAD_DIM), lambda b, h: (b, h)),
                  pl.BlockSpec((seq, HEAD_DIM), lambda b, h: (b, k_off + h)),
                  pl.BlockSpec((seq, HEAD_DIM), lambda b, h: (b, v_off + h)),
                  pl.BlockSpec((1, HEAD_DIM), lambda b, h: (0, 0))],
        out_specs=pl.BlockSpec((seq, HEAD_DIM), lambda b, h: (b, h)),
        out_shape=jax.ShapeDtypeStruct((m, heads * HEAD_DIM), BF16),
        scratch_shapes=[pltpu.VMEM((nq, acc_rows, t), BF16),
                        pltpu.VMEM((2, t, t), F32), pltpu.VMEM((2, t, t), F32),
                        pltpu.VMEM((2, 1, t), F32), pltpu.VMEM((2, 1, t), F32),
                        pltpu.VMEM((acc_rows, t), F32), pltpu.VMEM((acc_rows, t), F32)],
        compiler_params=pltpu.CompilerParams(
            dimension_semantics=("arbitrary", "arbitrary"),
            vmem_limit_bytes=VMEM_LIMIT_BYTES),
        name="diff_attention",
    )(lam_params, proj, proj, proj, subln_g.reshape(1, HEAD_DIM))


def _silu(g):
    return g * (1.0 / (1.0 + jnp.exp(-g)))


def _mix_out_kernel(*refs, conv, tm, tok_width, tiles_per_seq, mem_len):
    if conv:
        (xin_ref, gb_ref, gc_ref, cw_ref, qm_ref, gate_a_ref, gate_b_ref, kv_ref, wout_ref,
         x_ref, gpost_ref, o_ref, br_ref, u_ref) = refs
    else:
        (tok_ref, qm_ref, gate_a_ref, gate_b_ref, kv_ref, wout_ref,
         x_ref, gpost_ref, o_ref, br_ref) = refs
    half_gate = gate_a_ref.shape[1]

    def gate_cols(lo, hi):
        if hi <= half_gate:
            return _silu(gate_a_ref[:, lo:hi].astype(F32))
        assert lo >= half_gate
        return _silu(gate_b_ref[:, lo - half_gate:hi - half_gate].astype(F32))

    chunk = half_gate // 2
    if conv:
        pad = CONV_PAD
        first = (pl.program_id(0) % tiles_per_seq) == 0

        @pl.when(first)
        def _():
            u_ref[0:pad, :] = jnp.zeros((pad, tok_width), F32)

        @pl.when(jnp.logical_not(first))
        def _():
            u_ref[0:pad, :] = u_ref[tm:tm + pad, :]

        u_ref[pad:pad + tm, :] = gc_ref[...].astype(F32) * xin_ref[...].astype(F32)
        for lo in range(0, tok_width, chunk):
            hi = lo + chunk
            w = cw_ref[:, lo:hi]
            conv_out = (u_ref[pad:pad + tm, lo:hi] * w[2:3]
                        + u_ref[pad - 1:pad - 1 + tm, lo:hi] * w[1:2]
                        + u_ref[pad - 2:pad - 2 + tm, lo:hi] * w[0:1])
            tok = gb_ref[:, lo:hi].astype(F32) * conv_out
            br_ref[:, lo:hi] = (tok * gate_cols(lo, hi)).astype(BF16)
    else:
        for lo in range(0, tok_width, chunk):
            hi = lo + chunk
            br_ref[:, lo:hi] = (tok_ref[:, lo:hi].astype(F32) * gate_cols(lo, hi)).astype(BF16)

    scale = HEAD_DIM ** -0.5
    for h in range(CROSS_HEADS):
        lo = h * HEAD_DIM
        q_h = qm_ref[:, lo:lo + HEAD_DIM]
        k_h = kv_ref[:, lo:lo + HEAD_DIM]
        v_h = kv_ref[:, CROSS_WIDTH + lo:CROSS_WIDTH + lo + HEAD_DIM]
        s = lax.dot_general(q_h, k_h, (((1,), (1,)), ((), ())),
                            preferred_element_type=F32) * scale
        p = jnp.exp(s - jnp.max(s, axis=-1, keepdims=True))
        o_h = jnp.dot(p.astype(BF16), v_h, preferred_element_type=F32)
        o_h = o_h / jnp.sum(p, axis=-1, keepdims=True)
        col = tok_width + lo
        br_ref[:, col:col + HEAD_DIM] = (o_h * gate_cols(col, col + HEAD_DIM)).astype(BF16)

    y = jnp.dot(br_ref[...], wout_ref[...], preferred_element_type=F32)
    o_ref[...] = x_ref[...] + _rms_scale(y) * gpost_ref[...]


def _mix_out(proj, tok, conv_w, kv, w_out_bf16, x2d, g_post, *, seq, mem_len, tm):
    m, d = x2d.shape
    mix_width = w_out_bf16.shape[0]
    tok_width = mix_width - CROSS_WIDTH
    conv = tok is None
    tiles_per_seq = seq // tm
    qm_blk = (3 * tok_width) // CROSS_WIDTH
    half_gate = mix_width // 2
    gate_blk = (3 * tok_width + CROSS_WIDTH) // half_gate
    assert qm_blk * CROSS_WIDTH == 3 * tok_width
    assert gate_blk * half_gate == 3 * tok_width + CROSS_WIDTH

    tail_specs = [pl.BlockSpec((tm, CROSS_WIDTH), lambda i: (i, qm_blk)),
                  pl.BlockSpec((tm, half_gate), lambda i: (i, gate_blk)),
                  pl.BlockSpec((tm, half_gate), lambda i: (i, gate_blk + 1)),
                  pl.BlockSpec((mem_len, 2 * CROSS_WIDTH), lambda i: (i // tiles_per_seq, 0)),
                  pl.BlockSpec((mix_width, d), lambda i: (0, 0)),
                  pl.BlockSpec((tm, d), lambda i: (i, 0)),
                  pl.BlockSpec((1, d), lambda i: (0, 0))]
    tail_args = [proj, proj, proj, kv, w_out_bf16, x2d, g_post.reshape(1, d)]
    scratch = [pltpu.VMEM((tm, mix_width), BF16)]
    if conv:
        head_specs = [pl.BlockSpec((tm, tok_width), lambda i: (i, 0)),
                      pl.BlockSpec((tm, tok_width), lambda i: (i, 1)),
                      pl.BlockSpec((tm, tok_width), lambda i: (i, 2)),
                      pl.BlockSpec((CONV_WIDTH, tok_width), lambda i: (0, 0))]
        head_args = [proj, proj, proj, conv_w.T]
        scratch.append(pltpu.VMEM((tm + CONV_PAD, tok_width), F32))
    else:
        head_specs = [pl.BlockSpec((tm, tok_width), lambda i: (i, 0))]
        head_args = [tok]
    return pl.pallas_call(
        functools.partial(_mix_out_kernel, conv=conv, tm=tm, tok_width=tok_width,
                          tiles_per_seq=tiles_per_seq, mem_len=mem_len),
        grid=(m // tm,),
        in_specs=head_specs + tail_specs,
        out_specs=pl.BlockSpec((tm, d), lambda i: (i, 0)),
        out_shape=jax.ShapeDtypeStruct((m, d), F32),
        scratch_shapes=scratch,
        compiler_params=pltpu.CompilerParams(
            dimension_semantics=("arbitrary",), vmem_limit_bytes=VMEM_LIMIT_BYTES),
        name="mix_out_conv" if conv else "mix_out_attn",
    )(*head_args, *tail_args)


def kernel(x, mem, positions, pre_norm, post_norm, mem_norm, w_in, w_kv_mem, w_out,
           conv_w, diff_lambda, diff_subln):
    batch, seq, d = x.shape
    mem_len = mem.shape[1]
    depth = w_in.shape[0]
    tok_width = w_out.shape[1] - CROSS_WIDTH
    heads = tok_width // HEAD_DIM
    x2d = x.reshape(batch * seq, d)
    mem2d = mem.reshape(batch * mem_len, d)
    rope = _rope_tables(positions, tm=1024) if depth > 1 else None

    for i in range(depth):
        attn_layer = (i % N_MIXERS) == 1
        proj = _in_proj(x2d, pre_norm[i], w_in[i].astype(BF16), rope if attn_layer else None,
                        tm=1024, tn=1024, rope_cols=2 * tok_width)
        kv = _mem_kv(mem2d, mem_norm[i], w_kv_mem[i].astype(BF16), mem_len=mem_len)
        if attn_layer:
            tok = _diff_attention(proj, diff_lambda[i // N_MIXERS], diff_subln[i // N_MIXERS],
                                  batch=batch, seq=seq, heads=heads, layer_idx=i, t=512)
            conv = None
        else:
            tok = None
            conv = conv_w[i // N_MIXERS]
        x2d = _mix_out(proj, tok, conv, kv, w_out[i].astype(BF16), x2d, post_norm[i],
                       seq=seq, mem_len=mem_len, tm=256)
    return x2d.reshape(batch, seq, d)
```

```python
import functools
import math

import jax
import jax.numpy as jnp
from jax import lax
from jax.experimental import pallas as pl
from jax.experimental.pallas import tpu as pltpu

HEAD_DIM = 128
CROSS_HEADS = 4
CROSS_WIDTH = CROSS_HEADS * HEAD_DIM
CONV_WIDTH = 3
DIFF_HEAD_DIM = 64
ROPE_THETA = 10000.0
RMS_EPS = 1e-6
NEG_BIG = -1e30
N_MIXERS = 2
LOG2_E = math.log2(math.e)

LANES = 128
BF16_SUBLANES = 16
CONV_PAD = 8
VMEM_LIMIT_BYTES = 56 * 1024 * 1024

F32 = jnp.float32
BF16 = jnp.bfloat16


def _rms_scale(xf):
    return xf * lax.rsqrt(jnp.mean(xf * xf, axis=-1, keepdims=True) + RMS_EPS)


def _rope_table_kernel(pos_ref, inv_ref, sign_ref, cos_ref, sin_ref):
    ang = pos_ref[...].astype(F32) * inv_ref[...]
    cos_ref[...] = jnp.cos(ang)
    sin_ref[...] = jnp.sin(ang) * sign_ref[...]


def _rope_tables(positions, tm):
    m = positions.size
    half = DIFF_HEAD_DIM // 2
    inv_freq = ROPE_THETA ** (-jnp.arange(0, DIFF_HEAD_DIM, 2, dtype=F32) / DIFF_HEAD_DIM)
    reps = LANES // half
    inv = jnp.tile(inv_freq, reps).reshape(1, LANES)
    sign = jnp.tile(jnp.concatenate([-jnp.ones((half,), F32), jnp.ones((half,), F32)]),
                    LANES // DIFF_HEAD_DIM).reshape(1, LANES)
    pos_b = jnp.broadcast_to(positions.reshape(m, 1), (m, LANES))
    return pl.pallas_call(
        _rope_table_kernel,
        grid=(m // tm,),
        in_specs=[pl.BlockSpec((tm, LANES), lambda i: (i, 0)),
                  pl.BlockSpec((1, LANES), lambda i: (0, 0)),
                  pl.BlockSpec((1, LANES), lambda i: (0, 0))],
        out_specs=[pl.BlockSpec((tm, LANES), lambda i: (i, 0)),
                   pl.BlockSpec((tm, LANES), lambda i: (i, 0))],
        out_shape=[jax.ShapeDtypeStruct((m, LANES), F32),
                   jax.ShapeDtypeStruct((m, LANES), F32)],
        name="rope_tables",
    )(pos_b, inv, sign)


def _in_proj_kernel(*refs, rope_tiles, tn):
    if rope_tiles:
        x_ref, g_ref, w_ref, cos_ref, sin_ref, o_ref, h_ref = refs
    else:
        x_ref, g_ref, w_ref, o_ref, h_ref = refs
    j = pl.program_id(1)

    @pl.when(j == 0)
    def _():
        h_ref[...] = (_rms_scale(x_ref[...]) * g_ref[...]).astype(BF16)

    def project():
        return jnp.dot(h_ref[...], w_ref[...], preferred_element_type=F32)

    if not rope_tiles:
        o_ref[...] = project().astype(o_ref.dtype)
        return

    @pl.when(j < rope_tiles)
    def _():
        acc = project()
        cos = cos_ref[...]
        sin = sin_ref[...]
        lane = lax.broadcasted_iota(jnp.int32, cos.shape, 1)
        first_half = (lane % DIFF_HEAD_DIM) < (DIFF_HEAD_DIM // 2)
        half = DIFF_HEAD_DIM // 2
        for c in range(tn // LANES):
            t = acc[:, c * LANES:(c + 1) * LANES]
            partner = jnp.where(first_half, pltpu.roll(t, LANES - half, 1), pltpu.roll(t, half, 1))
            o_ref[:, c * LANES:(c + 1) * LANES] = (t * cos + partner * sin).astype(o_ref.dtype)

    @pl.when(j >= rope_tiles)
    def _():
        o_ref[...] = project().astype(o_ref.dtype)


def _in_proj(x2d, g_all, w_all_bf16, layer, rope, *, tm, tn, rope_cols):
    m, d = x2d.shape
    n = w_all_bf16.shape[2]
    rope_tiles = 0 if rope is None else rope_cols // tn
    in_specs = [pl.BlockSpec((tm, d), lambda i, j: (i, 0)),
                pl.BlockSpec((None, 1, d), lambda i, j: (layer, 0, 0)),
                pl.BlockSpec((None, d, tn), lambda i, j: (layer, 0, j))]
    args = [x2d, g_all, w_all_bf16]
    if rope is not None:
        assert rope_cols % tn == 0
        in_specs += [pl.BlockSpec((tm, LANES), lambda i, j: (i, 0)),
                     pl.BlockSpec((tm, LANES), lambda i, j: (i, 0))]
        args += list(rope)
    return pl.pallas_call(
        functools.partial(_in_proj_kernel, rope_tiles=rope_tiles, tn=tn),
        grid=(m // tm, n // tn),
        in_specs=in_specs,
        out_specs=pl.BlockSpec((tm, tn), lambda i, j: (i, j)),
        out_shape=jax.ShapeDtypeStruct((m, n), BF16),
        scratch_shapes=[pltpu.VMEM((tm, d), BF16)],
        compiler_params=pltpu.CompilerParams(
            dimension_semantics=("arbitrary", "arbitrary"),
            vmem_limit_bytes=VMEM_LIMIT_BYTES),
        name="in_proj_rope" if rope_tiles else "in_proj",
    )(*args)


def _mem_kv_kernel(mem_ref, g_ref, w_ref, o_ref):
    h = (_rms_scale(mem_ref[...]) * g_ref[...]).astype(BF16)
    o_ref[...] = jnp.dot(h, w_ref[...], preferred_element_type=F32).astype(o_ref.dtype)


def _mem_kv(mem2d, g_all, w_all_bf16, layer, *, mem_len):
    m, d = mem2d.shape
    n = w_all_bf16.shape[2]
    return pl.pallas_call(
        _mem_kv_kernel,
        grid=(m // mem_len,),
        in_specs=[pl.BlockSpec((mem_len, d), lambda i: (i, 0)),
                  pl.BlockSpec((None, 1, d), lambda i: (layer, 0, 0)),
                  pl.BlockSpec((None, d, n), lambda i: (layer, 0, 0))],
        out_specs=pl.BlockSpec((mem_len, n), lambda i: (i, 0)),
        out_shape=jax.ShapeDtypeStruct((m, n), BF16),
        compiler_params=pltpu.CompilerParams(
            dimension_semantics=("arbitrary",), vmem_limit_bytes=VMEM_LIMIT_BYTES),
        name="mem_kv",
    )(mem2d, g_all, w_all_bf16)


def _diff_attn_kernel(zero_ref, lam_ref, q_ref, k_ref, v_ref, g_ref, o_ref,
                      vt_ref, sa_ref, sb_ref, mxa_ref, mxb_ref, p_ref, acc1_ref, acc2_ref,
                      *, t, lambda_init):
    nq = vt_ref.shape[0]
    n_maps = 2
    acc_refs = (acc1_ref, acc2_ref)
    bufs = ((sa_ref, mxa_ref), (sb_ref, mxb_ref))
    z = zero_ref[0]
    half = t // 2

    ones = jnp.ones((BF16_SUBLANES, t), BF16)
    for j in range(nq):
        vt_ref[j, 0:HEAD_DIM, :] = v_ref[j * t:(j + 1) * t, :].astype(F32).T.astype(BF16)
        vt_ref[j, HEAD_DIM:, :] = ones

    lp = lam_ref[...]
    lam = (jnp.exp(jnp.sum(lp[0:1] * lp[1:2], axis=1, keepdims=True))
           - jnp.exp(jnp.sum(lp[2:3] * lp[3:4], axis=1, keepdims=True)) + lambda_init)

    def visible(n_keys, n_queries):
        key = lax.broadcasted_iota(jnp.int32, (n_keys, n_queries), 0)
        qry = lax.broadcasted_iota(jnp.int32, (n_keys, n_queries), 1)
        return key <= qry

    def q_maps_of(qi):
        q_t = (q_ref[qi * t:(qi + 1) * t, :].astype(F32) * (LOG2_E * DIFF_HEAD_DIM ** -0.5)).T
        row = lax.broadcasted_iota(jnp.int32, q_t.shape, 0)
        zero = jnp.zeros_like(q_t)
        return (jnp.where(row < DIFF_HEAD_DIM, q_t, zero).astype(BF16),
                jnp.where(row >= DIFF_HEAD_DIM, q_t, zero).astype(BF16))

    def scores(q_maps, j, diagonal, buf):
        s_ref, mx_ref = buf
        for mi in range(n_maps):
            if not diagonal:
                s = jnp.dot(k_ref[j * t:(j + 1) * t, :], q_maps[mi],
                            preferred_element_type=F32)
                s_ref[z + mi] = s
                mx_ref[mi] = jnp.max(s, axis=0, keepdims=True)
                continue
            s_top = jnp.dot(k_ref[j * t:j * t + half, :], q_maps[mi],
                            preferred_element_type=F32)
            s_top = jnp.where(visible(half, t), s_top, NEG_BIG)
            s_bot = jnp.dot(k_ref[j * t + half:(j + 1) * t, :], q_maps[mi][:, half:],
                            preferred_element_type=F32)
            s_bot = jnp.where(visible(half, half), s_bot, NEG_BIG)
            s_ref[z + mi, 0:half, :] = s_top
            s_ref[z + mi, half:, half:] = s_bot
            mx_top = jnp.max(s_top, axis=0, keepdims=True)
            mx_ref[mi, :, 0:half] = mx_top[:, :half]
            mx_ref[mi, :, half:] = jnp.maximum(mx_top[:, half:],
                                               jnp.max(s_bot, axis=0, keepdims=True))

    def consume(j, diagonal, buf, m_old):
        s_ref, mx_ref = buf
        v_t = vt_ref[j]
        m_out = []
        for mi in range(n_maps):
            acc_ref = acc_refs[mi]
            if m_old is None:
                m_new = mx_ref[mi]
                alpha = None
            else:
                m_new = jnp.maximum(m_old[mi], mx_ref[mi])
                alpha = jnp.exp2(m_old[mi] - m_new)
            if not diagonal:
                p_ref[z + mi] = jnp.exp2(s_ref[z + mi] - m_new).astype(BF16)
                pv = jnp.dot(v_t, p_ref[z + mi], preferred_element_type=F32)
                acc_ref[...] = pv if alpha is None else alpha * acc_ref[...] + pv
            else:
                p_ref[z + mi, 0:half, :] = jnp.exp2(
                    s_ref[z + mi, 0:half, :] - m_new).astype(BF16)
                p_ref[z + mi, half:, half:] = jnp.exp2(
                    s_ref[z + mi, half:, half:] - m_new[:, half:]).astype(BF16)
                pv = jnp.dot(v_t[:, :half], p_ref[z + mi, 0:half, :],
                             preferred_element_type=F32)
                pv_late = jnp.dot(v_t[:, half:], p_ref[z + mi, half:, half:],
                                  preferred_element_type=F32)
                if alpha is None:
                    acc_ref[:, :half] = pv[:, :half]
                    acc_ref[:, half:] = pv[:, half:] + pv_late
                else:
                    acc_ref[:, :half] = alpha[:, :half] * acc_ref[:, :half] + pv[:, :half]
                    acc_ref[:, half:] = (alpha[:, half:] * acc_ref[:, half:]
                                         + pv[:, half:] + pv_late)
            m_out.append(m_new)
        return m_out

    def finalize(qi):
        o1 = acc1_ref[0:HEAD_DIM, :] / acc1_ref[HEAD_DIM:HEAD_DIM + 1, :]
        o2 = acc2_ref[0:HEAD_DIM, :] / acc2_ref[HEAD_DIM:HEAD_DIM + 1, :]
        o = o1 - lam * o2
        o = o * lax.rsqrt(jnp.mean(o * o, axis=0, keepdims=True) + RMS_EPS)
        o_ref[qi * t:(qi + 1) * t, :] = (
            (o.T * g_ref[...]) * (1.0 - lambda_init)).astype(o_ref.dtype)

    tiles = [(qi, j) for qi in range(nq) for j in range(qi + 1)]
    q_maps = q_maps_of(0)
    scores(q_maps, 0, True, bufs[0])
    m_run = None
    for n, (qi, j) in enumerate(tiles):
        if n + 1 < len(tiles):
            qn, jn = tiles[n + 1]
            if qn != qi:
                q_maps = q_maps_of(qn)
            scores(q_maps, jn, jn == qn, bufs[(n + 1) % 2])
        m_run = consume(j, j == qi, bufs[n % 2], m_run)
        if j == qi:
            finalize(qi)
            m_run = None


def _diff_attention(proj, lam_params, subln_g, *, batch, seq, heads, layer_idx, t):
    m = proj.shape[0]
    nq = seq // t
    k_off = heads
    v_off = 2 * heads
    lambda_init = 0.8 - 0.6 * math.exp(-0.3 * layer_idx)
    acc_rows = HEAD_DIM + BF16_SUBLANES
    return pl.pallas_call(
        functools.partial(_diff_attn_kernel, t=t, lambda_init=lambda_init),
        grid=(batch, heads),
        in_specs=[pl.BlockSpec(memory_space=pltpu.SMEM),
                  pl.BlockSpec(lam_params.shape, lambda b, h: (0, 0)),
                  pl.BlockSpec((seq, HEAD_DIM), lambda b, h: (b, h)),
                  pl.BlockSpec((seq, HEAD_DIM), lambda b, h: (b, k_off + h)),
                  pl.BlockSpec((seq, HEAD_DIM), lambda b, h: (b, v_off + h)),
                  pl.BlockSpec((1, HEAD_DIM), lambda b, h: (0, 0))],
        out_specs=pl.BlockSpec((seq, HEAD_DIM), lambda b, h: (b, h)),
        out_shape=jax.ShapeDtypeStruct((m, heads * HEAD_DIM), BF16),
        scratch_shapes=[pltpu.VMEM((nq, acc_rows, t), BF16),
                        pltpu.VMEM((2, t, t), F32), pltpu.VMEM((2, t, t), F32),
                        pltpu.VMEM((2, 1, t), F32), pltpu.VMEM((2, 1, t), F32),
                        pltpu.VMEM((2, t, t), BF16),
                        pltpu.VMEM((acc_rows, t), F32), pltpu.VMEM((acc_rows, t), F32)],
        compiler_params=pltpu.CompilerParams(
            dimension_semantics=("arbitrary", "arbitrary"),
            vmem_limit_bytes=VMEM_LIMIT_BYTES),
        name="diff_attention",
    )(jnp.zeros((1,), jnp.int32), lam_params, proj, proj, proj, subln_g.reshape(1, HEAD_DIM))


def _silu(g):
    return g * (1.0 / (1.0 + jnp.exp(-g)))


def _mix_out_kernel(*refs, conv, tm, tok_width, tiles_per_seq, mem_len):
    if conv:
        (xin_ref, gb_ref, gc_ref, cw_ref, qm_ref, gate_a_ref, gate_b_ref, kv_ref, wout_ref,
         x_ref, gpost_ref, o_ref, br_ref, u_ref) = refs
    else:
        (tok_ref, qm_ref, gate_a_ref, gate_b_ref, kv_ref, wout_ref,
         x_ref, gpost_ref, o_ref, br_ref) = refs
    half_gate = gate_a_ref.shape[1]

    def gate_cols(lo, hi):
        if hi <= half_gate:
            return _silu(gate_a_ref[:, lo:hi].astype(F32))
        assert lo >= half_gate
        return _silu(gate_b_ref[:, lo - half_gate:hi - half_gate].astype(F32))

    chunk = half_gate // 2
    if conv:
        pad = CONV_PAD
        first = (pl.program_id(0) % tiles_per_seq) == 0

        @pl.when(first)
        def _():
            u_ref[0:pad, :] = jnp.zeros((pad, tok_width), F32)

        @pl.when(jnp.logical_not(first))
        def _():
            u_ref[0:pad, :] = u_ref[tm:tm + pad, :]

        u_ref[pad:pad + tm, :] = gc_ref[...].astype(F32) * xin_ref[...].astype(F32)
        for lo in range(0, tok_width, chunk):
            hi = lo + chunk
            w = cw_ref[:, lo:hi]
            conv_out = (u_ref[pad:pad + tm, lo:hi] * w[2:3]
                        + u_ref[pad - 1:pad - 1 + tm, lo:hi] * w[1:2]
                        + u_ref[pad - 2:pad - 2 + tm, lo:hi] * w[0:1])
            tok = gb_ref[:, lo:hi].astype(F32) * conv_out
            br_ref[:, lo:hi] = (tok * gate_cols(lo, hi)).astype(BF16)
    else:
        for lo in range(0, tok_width, chunk):
            hi = lo + chunk
            br_ref[:, lo:hi] = (tok_ref[:, lo:hi].astype(F32) * gate_cols(lo, hi)).astype(BF16)

    scale = HEAD_DIM ** -0.5
    for h in range(CROSS_HEADS):
        lo = h * HEAD_DIM
        q_h = qm_ref[:, lo:lo + HEAD_DIM]
        k_h = kv_ref[:, lo:lo + HEAD_DIM]
        v_h = kv_ref[:, CROSS_WIDTH + lo:CROSS_WIDTH + lo + HEAD_DIM]
        s = lax.dot_general(q_h, k_h, (((1,), (1,)), ((), ())),
                            preferred_element_type=F32) * scale
        p = jnp.exp(s - jnp.max(s, axis=-1, keepdims=True))
        o_h = jnp.dot(p.astype(BF16), v_h, preferred_element_type=F32)
        o_h = o_h / jnp.sum(p, axis=-1, keepdims=True)
        col = tok_width + lo
        br_ref[:, col:col + HEAD_DIM] = (o_h * gate_cols(col, col + HEAD_DIM)).astype(BF16)

    y = jnp.dot(br_ref[...], wout_ref[...], preferred_element_type=F32)
    o_ref[...] = x_ref[...] + _rms_scale(y) * gpost_ref[...]


def _mix_out(proj, tok, conv_w, kv, w_out_all_bf16, layer, x2d, g_post_all, *, seq, mem_len, tm):
    m, d = x2d.shape
    mix_width = w_out_all_bf16.shape[1]
    tok_width = mix_width - CROSS_WIDTH
    conv = tok is None
    tiles_per_seq = seq // tm
    qm_blk = (3 * tok_width) // CROSS_WIDTH
    half_gate = mix_width // 2
    gate_blk = (3 * tok_width + CROSS_WIDTH) // half_gate
    assert qm_blk * CROSS_WIDTH == 3 * tok_width
    assert gate_blk * half_gate == 3 * tok_width + CROSS_WIDTH

    tail_specs = [pl.BlockSpec((tm, CROSS_WIDTH), lambda i: (i, qm_blk)),
                  pl.BlockSpec((tm, half_gate), lambda i: (i, gate_blk)),
                  pl.BlockSpec((tm, half_gate), lambda i: (i, gate_blk + 1)),
                  pl.BlockSpec((mem_len, 2 * CROSS_WIDTH), lambda i: (i // tiles_per_seq, 0)),
                  pl.BlockSpec((None, mix_width, d), lambda i: (layer, 0, 0)),
                  pl.BlockSpec((tm, d), lambda i: (i, 0)),
                  pl.BlockSpec((None, 1, d), lambda i: (layer, 0, 0))]
    tail_args = [proj, proj, proj, kv, w_out_all_bf16, x2d, g_post_all]
    scratch = [pltpu.VMEM((tm, mix_width), BF16)]
    if conv:
        head_specs = [pl.BlockSpec((tm, tok_width), lambda i: (i, 0)),
                      pl.BlockSpec((tm, tok_width), lambda i: (i, 1)),
                      pl.BlockSpec((tm, tok_width), lambda i: (i, 2)),
                      pl.BlockSpec((CONV_WIDTH, tok_width), lambda i: (0, 0))]
        head_args = [proj, proj, proj, conv_w.T]
        scratch.append(pltpu.VMEM((tm + CONV_PAD, tok_width), F32))
    else:
        head_specs = [pl.BlockSpec((tm, tok_width), lambda i: (i, 0))]
        head_args = [tok]
    return pl.pallas_call(
        functools.partial(_mix_out_kernel, conv=conv, tm=tm, tok_width=tok_width,
                          tiles_per_seq=tiles_per_seq, mem_len=mem_len),
        grid=(m // tm,),
        in_specs=head_specs + tail_specs,
        out_specs=pl.BlockSpec((tm, d), lambda i: (i, 0)),
        out_shape=jax.ShapeDtypeStruct((m, d), F32),
        scratch_shapes=scratch,
        compiler_params=pltpu.CompilerParams(
            dimension_semantics=("arbitrary",), vmem_limit_bytes=VMEM_LIMIT_BYTES),
        name="mix_out_conv" if conv else "mix_out_attn",
    )(*head_args, *tail_args)


def kernel(x, mem, positions, pre_norm, post_norm, mem_norm, w_in, w_kv_mem, w_out,
           conv_w, diff_lambda, diff_subln):
    batch, seq, d = x.shape
    mem_len = mem.shape[1]
    depth = w_in.shape[0]
    tok_width = w_out.shape[1] - CROSS_WIDTH
    heads = tok_width // HEAD_DIM
    x2d = x.reshape(batch * seq, d)
    mem2d = mem.reshape(batch * mem_len, d)
    rope = _rope_tables(positions, tm=1024) if depth > 1 else None
    w_in_bf16 = w_in.astype(BF16)
    w_kv_bf16 = w_kv_mem.astype(BF16)
    w_out_bf16 = w_out.astype(BF16)
    pre_g = pre_norm.reshape(depth, 1, d)
    post_g = post_norm.reshape(depth, 1, d)
    mem_g = mem_norm.reshape(depth, 1, d)

    for i in range(depth):
        attn_layer = (i % N_MIXERS) == 1
        proj = _in_proj(x2d, pre_g, w_in_bf16, i, rope if attn_layer else None,
                        tm=1024, tn=1024, rope_cols=2 * tok_width)
        kv = _mem_kv(mem2d, mem_g, w_kv_bf16, i, mem_len=mem_len)
        if attn_layer:
            tok = _diff_attention(proj, diff_lambda[i // N_MIXERS], diff_subln[i // N_MIXERS],
                                  batch=batch, seq=seq, heads=heads, layer_idx=i, t=512)
            conv = None
        else:
            tok = None
            conv = conv_w[i // N_MIXERS]
        x2d = _mix_out(proj, tok, conv, kv, w_out_bf16, i, x2d, post_g,
                       seq=seq, mem_len=mem_len, tm=256)
    return x2d.reshape(batch, seq, d)
```

```python
import functools
import math

import jax
import jax.numpy as jnp
from jax import lax
from jax.experimental import pallas as pl
from jax.experimental.pallas import tpu as pltpu

HEAD_DIM = 128
CROSS_HEADS = 4
CROSS_WIDTH = CROSS_HEADS * HEAD_DIM
CONV_WIDTH = 3
DIFF_HEAD_DIM = 64
ROPE_THETA = 10000.0
RMS_EPS = 1e-6
NEG_BIG = -1e30
N_MIXERS = 2
LOG2_E = math.log2(math.e)

LANES = 128
BF16_SUBLANES = 16
CONV_PAD = 8
VMEM_LIMIT_BYTES = 56 * 1024 * 1024

F32 = jnp.float32
BF16 = jnp.bfloat16


def _rms_scale(xf):
    return xf * lax.rsqrt(jnp.mean(xf * xf, axis=-1, keepdims=True) + RMS_EPS)


def _rope_table_kernel(pos_ref, inv_ref, sign_ref, cos_ref, sin_ref):
    rows = pos_ref.shape[0]
    half = DIFF_HEAD_DIM // 2
    groups = LANES // half
    ang = pos_ref[...].astype(F32) * inv_ref[...]
    group_of_lane = lax.broadcasted_iota(jnp.int32, ang.shape, 1) // half
    for table, sign, out_ref in ((jnp.cos(ang), None, cos_ref),
                                 (jnp.sin(ang), sign_ref[...], sin_ref)):
        shifted = [table] + [pltpu.roll(table, d * half, 1) for d in range(1, groups)]
        for k in range(groups):
            g = shifted[(groups - k) % groups]
            for j in range(1, groups):
                g = jnp.where(group_of_lane == j, shifted[(j - k) % groups], g)
            out_ref[k * rows:(k + 1) * rows, :] = g if sign is None else g * sign


def _rope_tables(positions, tm):
    m = positions.size
    half = DIFF_HEAD_DIM // 2
    groups = LANES // half
    rows = tm // groups
    inv_freq = ROPE_THETA ** (-jnp.arange(0, DIFF_HEAD_DIM, 2, dtype=F32) / DIFF_HEAD_DIM)
    inv = jnp.tile(inv_freq, groups).reshape(1, LANES)
    sign = jnp.tile(jnp.concatenate([-jnp.ones((half,), F32), jnp.ones((half,), F32)]),
                    LANES // DIFF_HEAD_DIM).reshape(1, LANES)
    pos = positions.reshape(m // tm, groups, rows).transpose(0, 2, 1)
    pos = jnp.repeat(pos, half, axis=2).reshape(m // groups, LANES)
    return pl.pallas_call(
        _rope_table_kernel,
        grid=(m // tm,),
        in_specs=[pl.BlockSpec((rows, LANES), lambda i: (i, 0)),
                  pl.BlockSpec((1, LANES), lambda i: (0, 0)),
                  pl.BlockSpec((1, LANES), lambda i: (0, 0))],
        out_specs=[pl.BlockSpec((tm, LANES), lambda i: (i, 0)),
                   pl.BlockSpec((tm, LANES), lambda i: (i, 0))],
        out_shape=[jax.ShapeDtypeStruct((m, LANES), F32),
                   jax.ShapeDtypeStruct((m, LANES), F32)],
        name="rope_tables",
    )(pos, inv, sign)


def _in_proj_kernel(*refs, rope_tiles, tn):
    if rope_tiles:
        x_ref, g_ref, w_ref, cos_ref, sin_ref, o_ref, h_ref = refs
    else:
        x_ref, g_ref, w_ref, o_ref, h_ref = refs
    j = pl.program_id(1)

    @pl.when(j == 0)
    def _():
        h_ref[...] = (_rms_scale(x_ref[...]) * g_ref[...]).astype(BF16)

    def project():
        return jnp.dot(h_ref[...], w_ref[...], preferred_element_type=F32)

    if not rope_tiles:
        o_ref[...] = project().astype(o_ref.dtype)
        return

    @pl.when(j < rope_tiles)
    def _():
        acc = project()
        cos = cos_ref[...]
        sin = sin_ref[...]
        lane = lax.broadcasted_iota(jnp.int32, cos.shape, 1)
        first_half = (lane % DIFF_HEAD_DIM) < (DIFF_HEAD_DIM // 2)
        half = DIFF_HEAD_DIM // 2
        for c in range(tn // LANES):
            t = acc[:, c * LANES:(c + 1) * LANES]
            partner = jnp.where(first_half, pltpu.roll(t, LANES - half, 1), pltpu.roll(t, half, 1))
            o_ref[:, c * LANES:(c + 1) * LANES] = (t * cos + partner * sin).astype(o_ref.dtype)

    @pl.when(j >= rope_tiles)
    def _():
        o_ref[...] = project().astype(o_ref.dtype)


def _in_proj(x2d, g_all, w_all_bf16, layer, rope, *, tm, tn, rope_cols):
    m, d = x2d.shape
    n = w_all_bf16.shape[2]
    rope_tiles = 0 if rope is None else rope_cols // tn
    in_specs = [pl.BlockSpec((tm, d), lambda i, j: (i, 0)),
                pl.BlockSpec((None, 1, d), lambda i, j: (layer, 0, 0)),
                pl.BlockSpec((None, d, tn), lambda i, j: (layer, 0, j))]
    args = [x2d, g_all, w_all_bf16]
    if rope is not None:
        assert rope_cols % tn == 0
        in_specs += [pl.BlockSpec((tm, LANES), lambda i, j: (i, 0)),
                     pl.BlockSpec((tm, LANES), lambda i, j: (i, 0))]
        args += list(rope)
    return pl.pallas_call(
        functools.partial(_in_proj_kernel, rope_tiles=rope_tiles, tn=tn),
        grid=(m // tm, n // tn),
        in_specs=in_specs,
        out_specs=pl.BlockSpec((tm, tn), lambda i, j: (i, j)),
        out_shape=jax.ShapeDtypeStruct((m, n), BF16),
        scratch_shapes=[pltpu.VMEM((tm, d), BF16)],
        compiler_params=pltpu.CompilerParams(
            dimension_semantics=("arbitrary", "arbitrary"),
            vmem_limit_bytes=VMEM_LIMIT_BYTES),
        name="in_proj_rope" if rope_tiles else "in_proj",
    )(*args)


def _mem_kv_kernel(mem_ref, g_ref, w_ref, o_ref):
    h = (_rms_scale(mem_ref[...]) * g_ref[...]).astype(BF16)
    o_ref[...] = jnp.dot(h, w_ref[...], preferred_element_type=F32).astype(o_ref.dtype)


def _mem_kv(mem2d, g_all, w_all_bf16, layer, *, mem_len):
    m, d = mem2d.shape
    n = w_all_bf16.shape[2]
    return pl.pallas_call(
        _mem_kv_kernel,
        grid=(m // mem_len,),
        in_specs=[pl.BlockSpec((mem_len, d), lambda i: (i, 0)),
                  pl.BlockSpec((None, 1, d), lambda i: (layer, 0, 0)),
                  pl.BlockSpec((None, d, n), lambda i: (layer, 0, 0))],
        out_specs=pl.BlockSpec((mem_len, n), lambda i: (i, 0)),
        out_shape=jax.ShapeDtypeStruct((m, n), BF16),
        compiler_params=pltpu.CompilerParams(
            dimension_semantics=("arbitrary",), vmem_limit_bytes=VMEM_LIMIT_BYTES),
        name="mem_kv",
    )(mem2d, g_all, w_all_bf16)


def _diff_attn_kernel(zero_ref, lam_ref, q_ref, k_ref, v_ref, g_ref, o_ref,
                      vt_ref, sa_ref, sb_ref, mxa_ref, mxb_ref, p_ref, acc1_ref, acc2_ref,
                      *, t, lambda_init):
    nq = vt_ref.shape[0]
    n_maps = 2
    acc_refs = (acc1_ref, acc2_ref)
    bufs = ((sa_ref, mxa_ref), (sb_ref, mxb_ref))
    z = zero_ref[0]
    half = t // 2

    ones = jnp.ones((BF16_SUBLANES, t), BF16)
    for j in range(nq):
        vt_ref[j, 0:HEAD_DIM, :] = v_ref[j * t:(j + 1) * t, :].astype(F32).T.astype(BF16)
        vt_ref[j, HEAD_DIM:, :] = ones

    lp = lam_ref[...]
    lam = (jnp.exp(jnp.sum(lp[0:1] * lp[1:2], axis=1, keepdims=True))
           - jnp.exp(jnp.sum(lp[2:3] * lp[3:4], axis=1, keepdims=True)) + lambda_init)

    def visible(n_keys, n_queries):
        key = lax.broadcasted_iota(jnp.int32, (n_keys, n_queries), 0)
        qry = lax.broadcasted_iota(jnp.int32, (n_keys, n_queries), 1)
        return key <= qry

    def q_maps_of(qi):
        q_t = (q_ref[qi * t:(qi + 1) * t, :].astype(F32) * (LOG2_E * DIFF_HEAD_DIM ** -0.5)).T
        row = lax.broadcasted_iota(jnp.int32, q_t.shape, 0)
        zero = jnp.zeros_like(q_t)
        return (jnp.where(row < DIFF_HEAD_DIM, q_t, zero).astype(BF16),
                jnp.where(row >= DIFF_HEAD_DIM, q_t, zero).astype(BF16))

    def scores(q_maps, j, diagonal, buf):
        s_ref, mx_ref = buf
        for mi in range(n_maps):
            if not diagonal:
                s = jnp.dot(k_ref[j * t:(j + 1) * t, :], q_maps[mi],
                            preferred_element_type=F32)
                s_ref[z + mi] = s
                mx_ref[mi] = jnp.max(s, axis=0, keepdims=True)
                continue
            s_top = jnp.dot(k_ref[j * t:j * t + half, :], q_maps[mi],
                            preferred_element_type=F32)
            s_top = jnp.where(visible(half, t), s_top, NEG_BIG)
            s_bot = jnp.dot(k_ref[j * t + half:(j + 1) * t, :], q_maps[mi][:, half:],
                            preferred_element_type=F32)
            s_bot = jnp.where(visible(half, half), s_bot, NEG_BIG)
            s_ref[z + mi, 0:half, :] = s_top
            s_ref[z + mi, half:, half:] = s_bot
            mx_top = jnp.max(s_top, axis=0, keepdims=True)
            mx_ref[mi, :, 0:half] = mx_top[:, :half]
            mx_ref[mi, :, half:] = jnp.maximum(mx_top[:, half:],
                                               jnp.max(s_bot, axis=0, keepdims=True))

    def consume(j, diagonal, buf, m_old):
        s_ref, mx_ref = buf
        v_t = vt_ref[j]
        m_out = []
        for mi in range(n_maps):
            acc_ref = acc_refs[mi]
            if m_old is None:
                m_new = mx_ref[mi]
                alpha = None
            else:
                m_new = jnp.maximum(m_old[mi], mx_ref[mi])
                alpha = jnp.exp2(m_old[mi] - m_new)
            if not diagonal:
                p_ref[z + mi] = jnp.exp2(s_ref[z + mi] - m_new).astype(BF16)
                pv = jnp.dot(v_t, p_ref[z + mi], preferred_element_type=F32)
                acc_ref[...] = pv if alpha is None else alpha * acc_ref[...] + pv
            else:
                p_ref[z + mi, 0:half, :] = jnp.exp2(
                    s_ref[z + mi, 0:half, :] - m_new).astype(BF16)
                p_ref[z + mi, half:, half:] = jnp.exp2(
                    s_ref[z + mi, half:, half:] - m_new[:, half:]).astype(BF16)
                pv = jnp.dot(v_t[:, :half], p_ref[z + mi, 0:half, :],
                             preferred_element_type=F32)
                pv_late = jnp.dot(v_t[:, half:], p_ref[z + mi, half:, half:],
                                  preferred_element_type=F32)
                if alpha is None:
                    acc_ref[:, :half] = pv[:, :half]
                    acc_ref[:, half:] = pv[:, half:] + pv_late
                else:
                    acc_ref[:, :half] = alpha[:, :half] * acc_ref[:, :half] + pv[:, :half]
                    acc_ref[:, half:] = (alpha[:, half:] * acc_ref[:, half:]
                                         + pv[:, half:] + pv_late)
            m_out.append(m_new)
        return m_out

    def finalize(qi):
        o1 = acc1_ref[0:HEAD_DIM, :] / acc1_ref[HEAD_DIM:HEAD_DIM + 1, :]
        o2 = acc2_ref[0:HEAD_DIM, :] / acc2_ref[HEAD_DIM:HEAD_DIM + 1, :]
        o = o1 - lam * o2
        o = o * lax.rsqrt(jnp.mean(o * o, axis=0, keepdims=True) + RMS_EPS)
        o_ref[qi * t:(qi + 1) * t, :] = (
            (o.T * g_ref[...]) * (1.0 - lambda_init)).astype(o_ref.dtype)

    tiles = [(qi, j) for qi in range(nq) for j in range(qi + 1)]
    q_maps = q_maps_of(0)
    scores(q_maps, 0, True, bufs[0])
    m_run = None
    for n, (qi, j) in enumerate(tiles):
        if n + 1 < len(tiles):
            qn, jn = tiles[n + 1]
            if qn != qi:
                q_maps = q_maps_of(qn)
            scores(q_maps, jn, jn == qn, bufs[(n + 1) % 2])
        m_run = consume(j, j == qi, bufs[n % 2], m_run)
        if j == qi:
            finalize(qi)
            m_run = None


def _diff_attention(proj, lam_params, subln_g, *, batch, seq, heads, layer_idx, t):
    m = proj.shape[0]
    nq = seq // t
    k_off = heads
    v_off = 2 * heads
    lambda_init = 0.8 - 0.6 * math.exp(-0.3 * layer_idx)
    acc_rows = HEAD_DIM + BF16_SUBLANES
    return pl.pallas_call(
        functools.partial(_diff_attn_kernel, t=t, lambda_init=lambda_init),
        grid=(batch, heads),
        in_specs=[pl.BlockSpec(memory_space=pltpu.SMEM),
                  pl.BlockSpec(lam_params.shape, lambda b, h: (0, 0)),
                  pl.BlockSpec((seq, HEAD_DIM), lambda b, h: (b, h)),
                  pl.BlockSpec((seq, HEAD_DIM), lambda b, h: (b, k_off + h)),
                  pl.BlockSpec((seq, HEAD_DIM), lambda b, h: (b, v_off + h)),
                  pl.BlockSpec((1, HEAD_DIM), lambda b, h: (0, 0))],
        out_specs=pl.BlockSpec((seq, HEAD_DIM), lambda b, h: (b, h)),
        out_shape=jax.ShapeDtypeStruct((m, heads * HEAD_DIM), BF16),
        scratch_shapes=[pltpu.VMEM((nq, acc_rows, t), BF16),
                        pltpu.VMEM((2, t, t), F32), pltpu.VMEM((2, t, t), F32),
                        pltpu.VMEM((2, 1, t), F32), pltpu.VMEM((2, 1, t), F32),
                        pltpu.VMEM((2, t, t), BF16),
                        pltpu.VMEM((acc_rows, t), F32), pltpu.VMEM((acc_rows, t), F32)],
        compiler_params=pltpu.CompilerParams(
            dimension_semantics=("arbitrary", "arbitrary"),
            vmem_limit_bytes=VMEM_LIMIT_BYTES),
        name="diff_attention",
    )(jnp.zeros((1,), jnp.int32), lam_params, proj, proj, proj, subln_g.reshape(1, HEAD_DIM))


def _silu(g):
    return g * (1.0 / (1.0 + jnp.exp(-g)))


def _mix_out_kernel(*refs, conv, tm, tok_width, tiles_per_seq, mem_len):
    if conv:
        (xin_ref, gb_ref, gc_ref, cw_ref, qm_ref, gate_a_ref, gate_b_ref, kv_ref, wout_ref,
         x_ref, gpost_ref, o_ref, br_ref, u_ref) = refs
    else:
        (tok_ref, qm_ref, gate_a_ref, gate_b_ref, kv_ref, wout_ref,
         x_ref, gpost_ref, o_ref, br_ref) = refs
    half_gate = gate_a_ref.shape[1]

    def gate_cols(lo, hi):
        if hi <= half_gate:
            return _silu(gate_a_ref[:, lo:hi])
        assert lo >= half_gate
        return _silu(gate_b_ref[:, lo - half_gate:hi - half_gate])

    chunk = half_gate // 2
    if conv:
        pad = CONV_PAD
        first = (pl.program_id(0) % tiles_per_seq) == 0

        @pl.when(first)
        def _():
            u_ref[0:pad, :] = jnp.zeros((pad, tok_width), F32)

        @pl.when(jnp.logical_not(first))
        def _():
            u_ref[0:pad, :] = u_ref[tm:tm + pad, :]

        u_ref[pad:pad + tm, :] = gc_ref[...].astype(F32) * xin_ref[...].astype(F32)
        for lo in range(0, tok_width, chunk):
            hi = lo + chunk
            w = cw_ref[:, lo:hi]
            conv_out = (u_ref[pad:pad + tm, lo:hi] * w[2:3]
                        + u_ref[pad - 1:pad - 1 + tm, lo:hi] * w[1:2]
                        + u_ref[pad - 2:pad - 2 + tm, lo:hi] * w[0:1])
            tok = gb_ref[:, lo:hi].astype(F32) * conv_out
            br_ref[:, lo:hi] = tok.astype(BF16) * gate_cols(lo, hi)
    else:
        for lo in range(0, tok_width, chunk):
            hi = lo + chunk
            br_ref[:, lo:hi] = tok_ref[:, lo:hi] * gate_cols(lo, hi)

    scale = HEAD_DIM ** -0.5
    for h in range(CROSS_HEADS):
        lo = h * HEAD_DIM
        q_h = qm_ref[:, lo:lo + HEAD_DIM]
        k_h = kv_ref[:, lo:lo + HEAD_DIM]
        v_h = kv_ref[:, CROSS_WIDTH + lo:CROSS_WIDTH + lo + HEAD_DIM]
        s = lax.dot_general(q_h, k_h, (((1,), (1,)), ((), ())),
                            preferred_element_type=F32) * scale
        p = jnp.exp(s - jnp.max(s, axis=-1, keepdims=True))
        o_h = jnp.dot(p.astype(BF16), v_h, preferred_element_type=F32)
        o_h = o_h / jnp.sum(p, axis=-1, keepdims=True)
        col = tok_width + lo
        br_ref[:, col:col + HEAD_DIM] = o_h.astype(BF16) * gate_cols(col, col + HEAD_DIM)

    y = jnp.dot(br_ref[...], wout_ref[...], preferred_element_type=F32)
    o_ref[...] = x_ref[...] + _rms_scale(y) * gpost_ref[...]


def _mix_out(proj, tok, conv_w, kv, w_out_all_bf16, layer, x2d, g_post_all, *, seq, mem_len, tm):
    m, d = x2d.shape
    mix_width = w_out_all_bf16.shape[1]
    tok_width = mix_width - CROSS_WIDTH
    conv = tok is None
    tiles_per_seq = seq // tm
    qm_blk = (3 * tok_width) // CROSS_WIDTH
    half_gate = mix_width // 2
    gate_blk = (3 * tok_width + CROSS_WIDTH) // half_gate
    assert qm_blk * CROSS_WIDTH == 3 * tok_width
    assert gate_blk * half_gate == 3 * tok_width + CROSS_WIDTH

    tail_specs = [pl.BlockSpec((tm, CROSS_WIDTH), lambda i: (i, qm_blk)),
                  pl.BlockSpec((tm, half_gate), lambda i: (i, gate_blk)),
                  pl.BlockSpec((tm, half_gate), lambda i: (i, gate_blk + 1)),
                  pl.BlockSpec((mem_len, 2 * CROSS_WIDTH), lambda i: (i // tiles_per_seq, 0)),
                  pl.BlockSpec((None, mix_width, d), lambda i: (layer, 0, 0)),
                  pl.BlockSpec((tm, d), lambda i: (i, 0)),
                  pl.BlockSpec((None, 1, d), lambda i: (layer, 0, 0))]
    tail_args = [proj, proj, proj, kv, w_out_all_bf16, x2d, g_post_all]
    scratch = [pltpu.VMEM((tm, mix_width), BF16)]
    if conv:
        head_specs = [pl.BlockSpec((tm, tok_width), lambda i: (i, 0)),
                      pl.BlockSpec((tm, tok_width), lambda i: (i, 1)),
                      pl.BlockSpec((tm, tok_width), lambda i: (i, 2)),
                      pl.BlockSpec((CONV_WIDTH, tok_width), lambda i: (0, 0))]
        head_args = [proj, proj, proj, conv_w.T]
        scratch.append(pltpu.VMEM((tm + CONV_PAD, tok_width), F32))
    else:
        head_specs = [pl.BlockSpec((tm, tok_width), lambda i: (i, 0))]
        head_args = [tok]
    return pl.pallas_call(
        functools.partial(_mix_out_kernel, conv=conv, tm=tm, tok_width=tok_width,
                          tiles_per_seq=tiles_per_seq, mem_len=mem_len),
        grid=(m // tm,),
        in_specs=head_specs + tail_specs,
        out_specs=pl.BlockSpec((tm, d), lambda i: (i, 0)),
        out_shape=jax.ShapeDtypeStruct((m, d), F32),
        scratch_shapes=scratch,
        compiler_params=pltpu.CompilerParams(
            dimension_semantics=("arbitrary",), vmem_limit_bytes=VMEM_LIMIT_BYTES),
        name="mix_out_conv" if conv else "mix_out_attn",
    )(*head_args, *tail_args)


def kernel(x, mem, positions, pre_norm, post_norm, mem_norm, w_in, w_kv_mem, w_out,
           conv_w, diff_lambda, diff_subln):
    batch, seq, d = x.shape
    mem_len = mem.shape[1]
    depth = w_in.shape[0]
    tok_width = w_out.shape[1] - CROSS_WIDTH
    heads = tok_width // HEAD_DIM
    x2d = x.reshape(batch * seq, d)
    mem2d = mem.reshape(batch * mem_len, d)
    rope = _rope_tables(positions, tm=1024) if depth > 1 else None
    w_in_bf16 = w_in.astype(BF16)
    w_kv_bf16 = w_kv_mem.astype(BF16)
    w_out_bf16 = w_out.astype(BF16)
    pre_g = pre_norm.reshape(depth, 1, d)
    post_g = post_norm.reshape(depth, 1, d)
    mem_g = mem_norm.reshape(depth, 1, d)

    for i in range(depth):
        attn_layer = (i % N_MIXERS) == 1
        proj = _in_proj(x2d, pre_g, w_in_bf16, i, rope if attn_layer else None,
                        tm=1024, tn=1024, rope_cols=2 * tok_width)
        kv = _mem_kv(mem2d, mem_g, w_kv_bf16, i, mem_len=mem_len)
        if attn_layer:
            tok = _diff_attention(proj, diff_lambda[i // N_MIXERS], diff_subln[i // N_MIXERS],
                                  batch=batch, seq=seq, heads=heads, layer_idx=i, t=512)
            conv = None
        else:
            tok = None
            conv = conv_w[i // N_MIXERS]
        x2d = _mix_out(proj, tok, conv, kv, w_out_bf16, i, x2d, post_g,
                       seq=seq, mem_len=mem_len, tm=256)
    return x2d.reshape(batch, seq, d)
```

```python
import functools
import math

import jax
import jax.numpy as jnp
from jax import lax
from jax.experimental import pallas as pl
from jax.experimental.pallas import tpu as pltpu

HEAD_DIM = 128
CROSS_HEADS = 4
CROSS_WIDTH = CROSS_HEADS * HEAD_DIM
CONV_WIDTH = 3
DIFF_HEAD_DIM = 64
ROPE_THETA = 10000.0
RMS_EPS = 1e-6
NEG_BIG = -1e30
N_MIXERS = 2
LOG2_E = math.log2(math.e)

LANES = 128
BF16_SUBLANES = 16
CONV_PAD = 8
MXU_DEPTH = 256
ROW_PARTS = 2
VMEM_LIMIT_BYTES = 56 * 1024 * 1024

F32 = jnp.float32
BF16 = jnp.bfloat16


def _rms_scale(xf):
    return xf * lax.rsqrt(jnp.mean(xf * xf, axis=-1, keepdims=True) + RMS_EPS)


def _rope_table_kernel(pos_ref, inv_ref, sign_ref, cos_ref, sin_ref):
    rows = pos_ref.shape[0]
    half = DIFF_HEAD_DIM // 2
    groups = LANES // half
    ang = pos_ref[...].astype(F32) * inv_ref[...]
    group_of_lane = lax.broadcasted_iota(jnp.int32, ang.shape, 1) // half
    for table, sign, out_ref in ((jnp.cos(ang), None, cos_ref),
                                 (jnp.sin(ang), sign_ref[...], sin_ref)):
        shifted = [table] + [pltpu.roll(table, d * half, 1) for d in range(1, groups)]
        for k in range(groups):
            g = shifted[(groups - k) % groups]
            for j in range(1, groups):
                g = jnp.where(group_of_lane == j, shifted[(j - k) % groups], g)
            out_ref[k * rows:(k + 1) * rows, :] = g if sign is None else g * sign


def _rope_tables(positions, tm):
    m = positions.size
    half = DIFF_HEAD_DIM // 2
    groups = LANES // half
    rows = tm // groups
    inv_freq = ROPE_THETA ** (-jnp.arange(0, DIFF_HEAD_DIM, 2, dtype=F32) / DIFF_HEAD_DIM)
    inv = jnp.tile(inv_freq, groups).reshape(1, LANES)
    sign = jnp.tile(jnp.concatenate([-jnp.ones((half,), F32), jnp.ones((half,), F32)]),
                    LANES // DIFF_HEAD_DIM).reshape(1, LANES)
    pos = positions.reshape(m // tm, groups, rows).transpose(0, 2, 1)
    pos = jnp.repeat(pos, half, axis=2).reshape(m // groups, LANES)
    return pl.pallas_call(
        _rope_table_kernel,
        grid=(m // tm,),
        in_specs=[pl.BlockSpec((rows, LANES), lambda i: (i, 0)),
                  pl.BlockSpec((1, LANES), lambda i: (0, 0)),
                  pl.BlockSpec((1, LANES), lambda i: (0, 0))],
        out_specs=[pl.BlockSpec((tm, LANES), lambda i: (i, 0)),
                   pl.BlockSpec((tm, LANES), lambda i: (i, 0))],
        out_shape=[jax.ShapeDtypeStruct((m, LANES), F32),
                   jax.ShapeDtypeStruct((m, LANES), F32)],
        name="rope_tables",
    )(pos, inv, sign)


def _in_proj_kernel(*refs, rope_tiles, tn):
    if rope_tiles:
        x_ref, g_ref, w_ref, cos_ref, sin_ref, o_ref, h_ref = refs
    else:
        x_ref, g_ref, w_ref, o_ref, h_ref = refs
    j = pl.program_id(1)

    @pl.when(j == 0)
    def _():
        h_ref[...] = (_rms_scale(x_ref[...]) * g_ref[...]).astype(BF16)

    def project():
        return jnp.dot(h_ref[...], w_ref[...], preferred_element_type=F32)

    if not rope_tiles:
        o_ref[...] = project().astype(o_ref.dtype)
        return

    @pl.when(j < rope_tiles)
    def _():
        acc = project()
        cos = cos_ref[...]
        sin = sin_ref[...]
        lane = lax.broadcasted_iota(jnp.int32, cos.shape, 1)
        first_half = (lane % DIFF_HEAD_DIM) < (DIFF_HEAD_DIM // 2)
        half = DIFF_HEAD_DIM // 2
        for c in range(tn // LANES):
            t = acc[:, c * LANES:(c + 1) * LANES]
            partner = jnp.where(first_half, pltpu.roll(t, LANES - half, 1), pltpu.roll(t, half, 1))
            o_ref[:, c * LANES:(c + 1) * LANES] = (t * cos + partner * sin).astype(o_ref.dtype)

    @pl.when(j >= rope_tiles)
    def _():
        o_ref[...] = project().astype(o_ref.dtype)


def _in_proj(x2d, g_all, w_all_bf16, layer, rope, *, tm, tn, rope_cols):
    m, d = x2d.shape
    n = w_all_bf16.shape[2]
    rope_tiles = 0 if rope is None else rope_cols // tn
    in_specs = [pl.BlockSpec((tm, d), lambda i, j: (i, 0)),
                pl.BlockSpec((None, 1, d), lambda i, j: (layer, 0, 0)),
                pl.BlockSpec((None, d, tn), lambda i, j: (layer, 0, j))]
    args = [x2d, g_all, w_all_bf16]
    if rope is not None:
        assert rope_cols % tn == 0
        in_specs += [pl.BlockSpec((tm, LANES), lambda i, j: (i, 0)),
                     pl.BlockSpec((tm, LANES), lambda i, j: (i, 0))]
        args += list(rope)
    return pl.pallas_call(
        functools.partial(_in_proj_kernel, rope_tiles=rope_tiles, tn=tn),
        grid=(m // tm, n // tn),
        in_specs=in_specs,
        out_specs=pl.BlockSpec((tm, tn), lambda i, j: (i, j)),
        out_shape=jax.ShapeDtypeStruct((m, n), BF16),
        scratch_shapes=[pltpu.VMEM((tm, d), BF16)],
        compiler_params=pltpu.CompilerParams(
            dimension_semantics=("arbitrary", "arbitrary"),
            vmem_limit_bytes=VMEM_LIMIT_BYTES),
        name="in_proj_rope" if rope_tiles else "in_proj",
    )(*args)


def _mem_kv_kernel(mem_ref, g_ref, w_ref, o_ref):
    h = (_rms_scale(mem_ref[...]) * g_ref[...]).astype(BF16)
    o_ref[...] = jnp.dot(h, w_ref[...], preferred_element_type=F32).astype(o_ref.dtype)


def _mem_kv(mem2d, g_all, w_all_bf16, layer, *, mem_len):
    m, d = mem2d.shape
    n = w_all_bf16.shape[2]
    return pl.pallas_call(
        _mem_kv_kernel,
        grid=(m // mem_len,),
        in_specs=[pl.BlockSpec((mem_len, d), lambda i: (i, 0)),
                  pl.BlockSpec((None, 1, d), lambda i: (layer, 0, 0)),
                  pl.BlockSpec((None, d, n), lambda i: (layer, 0, 0))],
        out_specs=pl.BlockSpec((mem_len, n), lambda i: (i, 0)),
        out_shape=jax.ShapeDtypeStruct((m, n), BF16),
        compiler_params=pltpu.CompilerParams(
            dimension_semantics=("arbitrary",), vmem_limit_bytes=VMEM_LIMIT_BYTES),
        name="mem_kv",
    )(mem2d, g_all, w_all_bf16)


def _diff_attn_kernel(zero_ref, lam_ref, q_ref, k_ref, v_ref, g_ref, o_ref,
                      vt_ref, sa_ref, sb_ref, mxa_ref, mxb_ref, p_ref, acc1_ref, acc2_ref,
                      *, t, lambda_init):
    nq = vt_ref.shape[0]
    n_maps = 2
    acc_refs = (acc1_ref, acc2_ref)
    bufs = ((sa_ref, mxa_ref), (sb_ref, mxb_ref))
    z = zero_ref[0]
    half = t // 2

    ones = jnp.ones((BF16_SUBLANES, t), BF16)
    for j in range(nq):
        vt_ref[j, 0:HEAD_DIM, :] = v_ref[j * t:(j + 1) * t, :].astype(F32).T.astype(BF16)
        vt_ref[j, HEAD_DIM:, :] = ones

    lp = lam_ref[...]
    lam = (jnp.exp(jnp.sum(lp[0:1] * lp[1:2], axis=1, keepdims=True))
           - jnp.exp(jnp.sum(lp[2:3] * lp[3:4], axis=1, keepdims=True)) + lambda_init)

    def visible(n_keys, n_queries):
        key = lax.broadcasted_iota(jnp.int32, (n_keys, n_queries), 0)
        qry = lax.broadcasted_iota(jnp.int32, (n_keys, n_queries), 1)
        return key <= qry

    def q_maps_of(qi):
        q_t = (q_ref[qi * t:(qi + 1) * t, :].astype(F32) * (LOG2_E * DIFF_HEAD_DIM ** -0.5)).T
        row = lax.broadcasted_iota(jnp.int32, q_t.shape, 0)
        zero = jnp.zeros_like(q_t)
        return (jnp.where(row < DIFF_HEAD_DIM, q_t, zero).astype(BF16),
                jnp.where(row >= DIFF_HEAD_DIM, q_t, zero).astype(BF16))

    def scores(q_maps, j, diagonal, buf):
        s_ref, mx_ref = buf
        for mi in range(n_maps):
            if not diagonal:
                s = jnp.dot(k_ref[j * t:(j + 1) * t, :], q_maps[mi],
                            preferred_element_type=F32)
                s_ref[z + mi] = s
                mx_ref[mi] = jnp.max(s, axis=0, keepdims=True)
                continue
            s_top = jnp.dot(k_ref[j * t:j * t + half, :], q_maps[mi],
                            preferred_element_type=F32)
            s_top = jnp.where(visible(half, t), s_top, NEG_BIG)
            s_bot = jnp.dot(k_ref[j * t + half:(j + 1) * t, :], q_maps[mi][:, half:],
                            preferred_element_type=F32)
            s_bot = jnp.where(visible(half, half), s_bot, NEG_BIG)
            s_ref[z + mi, 0:half, :] = s_top
            s_ref[z + mi, half:, half:] = s_bot
            mx_top = jnp.max(s_top, axis=0, keepdims=True)
            mx_ref[mi, :, 0:half] = mx_top[:, :half]
            mx_ref[mi, :, half:] = jnp.maximum(mx_top[:, half:],
                                               jnp.max(s_bot, axis=0, keepdims=True))

    def consume(j, diagonal, buf, m_old):
        s_ref, mx_ref = buf
        v_t = vt_ref[j]
        m_out = []
        for mi in range(n_maps):
            acc_ref = acc_refs[mi]
            if m_old is None:
                m_new = mx_ref[mi]
                alpha = None
            else:
                m_new = jnp.maximum(m_old[mi], mx_ref[mi])
                alpha = jnp.exp2(m_old[mi] - m_new)
            if not diagonal:
                p_ref[z + mi] = jnp.exp2(s_ref[z + mi] - m_new).astype(BF16)
                pv = jnp.dot(v_t, p_ref[z + mi], preferred_element_type=F32)
                acc_ref[...] = pv if alpha is None else alpha * acc_ref[...] + pv
            else:
                p_ref[z + mi, 0:half, :] = jnp.exp2(
                    s_ref[z + mi, 0:half, :] - m_new).astype(BF16)
                p_ref[z + mi, half:, half:] = jnp.exp2(
                    s_ref[z + mi, half:, half:] - m_new[:, half:]).astype(BF16)
                pv = jnp.dot(v_t[:, :half], p_ref[z + mi, 0:half, :],
                             preferred_element_type=F32)
                pv_late = jnp.dot(v_t[:, half:], p_ref[z + mi, half:, half:],
                                  preferred_element_type=F32)
                if alpha is None:
                    acc_ref[:, :half] = pv[:, :half]
                    acc_ref[:, half:] = pv[:, half:] + pv_late
                else:
                    acc_ref[:, :half] = alpha[:, :half] * acc_ref[:, :half] + pv[:, :half]
                    acc_ref[:, half:] = (alpha[:, half:] * acc_ref[:, half:]
                                         + pv[:, half:] + pv_late)
            m_out.append(m_new)
        return m_out

    def finalize(qi):
        o1 = acc1_ref[0:HEAD_DIM, :] / acc1_ref[HEAD_DIM:HEAD_DIM + 1, :]
        o2 = acc2_ref[0:HEAD_DIM, :] / acc2_ref[HEAD_DIM:HEAD_DIM + 1, :]
        o = o1 - lam * o2
        o = o * lax.rsqrt(jnp.mean(o * o, axis=0, keepdims=True) + RMS_EPS)
        o_ref[qi * t:(qi + 1) * t, :] = (
            (o.T * g_ref[...]) * (1.0 - lambda_init)).astype(o_ref.dtype)

    tiles = [(qi, j) for qi in range(nq) for j in range(qi + 1)]
    q_maps = q_maps_of(0)
    scores(q_maps, 0, True, bufs[0])
    m_run = None
    for n, (qi, j) in enumerate(tiles):
        if n + 1 < len(tiles):
            qn, jn = tiles[n + 1]
            if qn != qi:
                q_maps = q_maps_of(qn)
            scores(q_maps, jn, jn == qn, bufs[(n + 1) % 2])
        m_run = consume(j, j == qi, bufs[n % 2], m_run)
        if j == qi:
            finalize(qi)
            m_run = None


def _diff_attention(proj, lam_params, subln_g, *, batch, seq, heads, layer_idx, t):
    m = proj.shape[0]
    nq = seq // t
    k_off = heads
    v_off = 2 * heads
    lambda_init = 0.8 - 0.6 * math.exp(-0.3 * layer_idx)
    acc_rows = HEAD_DIM + BF16_SUBLANES
    return pl.pallas_call(
        functools.partial(_diff_attn_kernel, t=t, lambda_init=lambda_init),
        grid=(batch, heads),
        in_specs=[pl.BlockSpec(memory_space=pltpu.SMEM),
                  pl.BlockSpec(lam_params.shape, lambda b, h: (0, 0)),
                  pl.BlockSpec((seq, HEAD_DIM), lambda b, h: (b, h)),
                  pl.BlockSpec((seq, HEAD_DIM), lambda b, h: (b, k_off + h)),
                  pl.BlockSpec((seq, HEAD_DIM), lambda b, h: (b, v_off + h)),
                  pl.BlockSpec((1, HEAD_DIM), lambda b, h: (0, 0))],
        out_specs=pl.BlockSpec((seq, HEAD_DIM), lambda b, h: (b, h)),
        out_shape=jax.ShapeDtypeStruct((m, heads * HEAD_DIM), BF16),
        scratch_shapes=[pltpu.VMEM((nq, acc_rows, t), BF16),
                        pltpu.VMEM((2, t, t), F32), pltpu.VMEM((2, t, t), F32),
                        pltpu.VMEM((2, 1, t), F32), pltpu.VMEM((2, 1, t), F32),
                        pltpu.VMEM((2, t, t), BF16),
                        pltpu.VMEM((acc_rows, t), F32), pltpu.VMEM((acc_rows, t), F32)],
        compiler_params=pltpu.CompilerParams(
            dimension_semantics=("arbitrary", "arbitrary"),
            vmem_limit_bytes=VMEM_LIMIT_BYTES),
        name="diff_attention",
    )(jnp.zeros((1,), jnp.int32), lam_params, proj, proj, proj, subln_g.reshape(1, HEAD_DIM))


def _silu(g):
    return g * (1.0 / (1.0 + jnp.exp(-g)))


def _mix_out_kernel(*refs, conv, tm, tok_width, tiles_per_seq, mem_len):
    if conv:
        (xin_ref, gb_ref, gc_ref, cw_ref, qm_ref, gate_a_ref, gate_b_ref, kv_ref, wout_ref,
         x_ref, gpost_ref, o_ref, br_ref, y_ref, u_ref) = refs
    else:
        (tok_ref, qm_ref, gate_a_ref, gate_b_ref, kv_ref, wout_ref,
         x_ref, gpost_ref, o_ref, br_ref, y_ref) = refs
    half_gate = gate_a_ref.shape[1]
    chunk = MXU_DEPTH
    pad = CONV_PAD

    def gate_cols(rows, lo, hi):
        if hi <= half_gate:
            return _silu(gate_a_ref[rows, lo:hi])
        assert lo >= half_gate
        return _silu(gate_b_ref[rows, lo - half_gate:hi - half_gate])

    if conv:
        first = (pl.program_id(0) % tiles_per_seq) == 0

        @pl.when(first)
        def _():
            u_ref[0:pad, :] = jnp.zeros((pad, tok_width), F32)

        @pl.when(jnp.logical_not(first))
        def _():
            u_ref[0:pad, :] = u_ref[tm:tm + pad, :]

        u_ref[pad:pad + tm, :] = gc_ref[...].astype(F32) * xin_ref[...].astype(F32)

    def rows_part(r0, r1):
        rows = slice(r0, r1)

        def project(lo, hi, first_chunk):
            part = jnp.dot(br_ref[rows, lo:hi], wout_ref[lo:hi, :], preferred_element_type=F32)
            if first_chunk:
                y_ref[rows, :] = part
            else:
                y_ref[rows, :] += part

        for lo in range(0, tok_width, chunk):
            hi = lo + chunk
            if conv:
                w = cw_ref[:, lo:hi]
                conv_out = (u_ref[pad + r0:pad + r1, lo:hi] * w[2:3]
                            + u_ref[pad - 1 + r0:pad - 1 + r1, lo:hi] * w[1:2]
                            + u_ref[pad - 2 + r0:pad - 2 + r1, lo:hi] * w[0:1])
                tok = (gb_ref[rows, lo:hi].astype(F32) * conv_out).astype(BF16)
            else:
                tok = tok_ref[rows, lo:hi]
            br_ref[rows, lo:hi] = tok * gate_cols(rows, lo, hi)
            project(lo, hi, lo == 0)

        scale = HEAD_DIM ** -0.5
        for h in range(CROSS_HEADS):
            lo = h * HEAD_DIM
            q_h = qm_ref[rows, lo:lo + HEAD_DIM]
            k_h = kv_ref[:, lo:lo + HEAD_DIM]
            v_h = kv_ref[:, CROSS_WIDTH + lo:CROSS_WIDTH + lo + HEAD_DIM]
            s = lax.dot_general(q_h, k_h, (((1,), (1,)), ((), ())),
                                preferred_element_type=F32) * scale
            p = jnp.exp(s - jnp.max(s, axis=-1, keepdims=True))
            o_h = jnp.dot(p.astype(BF16), v_h, preferred_element_type=F32)
            o_h = o_h / jnp.sum(p, axis=-1, keepdims=True)
            col = tok_width + lo
            br_ref[rows, col:col + HEAD_DIM] = (
                o_h.astype(BF16) * gate_cols(rows, col, col + HEAD_DIM))
        for lo in range(tok_width, tok_width + CROSS_WIDTH, chunk):
            project(lo, lo + chunk, False)

        o_ref[rows, :] = x_ref[rows, :] + _rms_scale(y_ref[rows, :]) * gpost_ref[...]

    part_rows = tm // ROW_PARTS
    for r0 in range(0, tm, part_rows):
        rows_part(r0, r0 + part_rows)


def _mix_out(proj, tok, conv_w, kv, w_out_all_bf16, layer, x2d, g_post_all, *, seq, mem_len, tm):
    m, d = x2d.shape
    mix_width = w_out_all_bf16.shape[1]
    tok_width = mix_width - CROSS_WIDTH
    conv = tok is None
    tiles_per_seq = seq // tm
    qm_blk = (3 * tok_width) // CROSS_WIDTH
    half_gate = mix_width // 2
    gate_blk = (3 * tok_width + CROSS_WIDTH) // half_gate
    assert qm_blk * CROSS_WIDTH == 3 * tok_width
    assert gate_blk * half_gate == 3 * tok_width + CROSS_WIDTH

    tail_specs = [pl.BlockSpec((tm, CROSS_WIDTH), lambda i: (i, qm_blk)),
                  pl.BlockSpec((tm, half_gate), lambda i: (i, gate_blk)),
                  pl.BlockSpec((tm, half_gate), lambda i: (i, gate_blk + 1)),
                  pl.BlockSpec((mem_len, 2 * CROSS_WIDTH), lambda i: (i // tiles_per_seq, 0)),
                  pl.BlockSpec((None, mix_width, d), lambda i: (layer, 0, 0),
                               pipeline_mode=pl.Buffered(1)),
                  pl.BlockSpec((tm, d), lambda i: (i, 0)),
                  pl.BlockSpec((None, 1, d), lambda i: (layer, 0, 0))]
    tail_args = [proj, proj, proj, kv, w_out_all_bf16, x2d, g_post_all]
    scratch = [pltpu.VMEM((tm, mix_width), BF16), pltpu.VMEM((tm, d), F32)]
    if conv:
        head_specs = [pl.BlockSpec((tm, tok_width), lambda i: (i, 0)),
                      pl.BlockSpec((tm, tok_width), lambda i: (i, 1)),
                      pl.BlockSpec((tm, tok_width), lambda i: (i, 2)),
                      pl.BlockSpec((CONV_WIDTH, tok_width), lambda i: (0, 0))]
        head_args = [proj, proj, proj, conv_w.T]
        scratch.append(pltpu.VMEM((tm + CONV_PAD, tok_width), F32))
    else:
        head_specs = [pl.BlockSpec((tm, tok_width), lambda i: (i, 0))]
        head_args = [tok]
    return pl.pallas_call(
        functools.partial(_mix_out_kernel, conv=conv, tm=tm, tok_width=tok_width,
                          tiles_per_seq=tiles_per_seq, mem_len=mem_len),
        grid=(m // tm,),
        in_specs=head_specs + tail_specs,
        out_specs=pl.BlockSpec((tm, d), lambda i: (i, 0)),
        out_shape=jax.ShapeDtypeStruct((m, d), F32),
        scratch_shapes=scratch,
        compiler_params=pltpu.CompilerParams(
            dimension_semantics=("arbitrary",), vmem_limit_bytes=VMEM_LIMIT_BYTES),
        name="mix_out_conv" if conv else "mix_out_attn",
    )(*head_args, *tail_args)


def kernel(x, mem, positions, pre_norm, post_norm, mem_norm, w_in, w_kv_mem, w_out,
           conv_w, diff_lambda, diff_subln):
    batch, seq, d = x.shape
    mem_len = mem.shape[1]
    depth = w_in.shape[0]
    tok_width = w_out.shape[1] - CROSS_WIDTH
    heads = tok_width // HEAD_DIM
    x2d = x.reshape(batch * seq, d)
    mem2d = mem.reshape(batch * mem_len, d)
    rope = _rope_tables(positions, tm=1024) if depth > 1 else None
    w_in_bf16 = w_in.astype(BF16)
    w_kv_bf16 = w_kv_mem.astype(BF16)
    w_out_bf16 = w_out.astype(BF16)
    pre_g = pre_norm.reshape(depth, 1, d)
    post_g = post_norm.reshape(depth, 1, d)
    mem_g = mem_norm.reshape(depth, 1, d)

    for i in range(depth):
        attn_layer = (i % N_MIXERS) == 1
        proj = _in_proj(x2d, pre_g, w_in_bf16, i, rope if attn_layer else None,
                        tm=1024, tn=1024, rope_cols=2 * tok_width)
        kv = _mem_kv(mem2d, mem_g, w_kv_bf16, i, mem_len=mem_len)
        if attn_layer:
            tok = _diff_attention(proj, diff_lambda[i // N_MIXERS], diff_subln[i // N_MIXERS],
                                  batch=batch, seq=seq, heads=heads, layer_idx=i, t=512)
            conv = None
        else:
            tok = None
            conv = conv_w[i // N_MIXERS]
        x2d = _mix_out(proj, tok, conv, kv, w_out_bf16, i, x2d, post_g,
                       seq=seq, mem_len=mem_len, tm=512)
    return x2d.reshape(batch, seq, d)
```

```python
import functools
import math

import jax
import jax.numpy as jnp
from jax import lax
from jax.experimental import pallas as pl
from jax.experimental.pallas import tpu as pltpu

HEAD_DIM = 128
CROSS_HEADS = 4
CROSS_WIDTH = CROSS_HEADS * HEAD_DIM
CONV_WIDTH = 3
DIFF_HEAD_DIM = 64
ROPE_THETA = 10000.0
RMS_EPS = 1e-6
NEG_BIG = -1e30
N_MIXERS = 2
LOG2_E = math.log2(math.e)

LANES = 128
BF16_SUBLANES = 16
CONV_PAD = 8
MXU_DEPTH = 256
KV_STEP = 1
ROW_PARTS = 2
VMEM_LIMIT_BYTES = 56 * 1024 * 1024

IN_PROJ_ROWS = 1024
IN_PROJ_COLS = 1024
ATTN_TILE = 512
MIX_ROWS = 512

F32 = jnp.float32
BF16 = jnp.bfloat16


def _rms_scale(xf):
    return xf * lax.rsqrt(jnp.mean(xf * xf, axis=-1, keepdims=True) + RMS_EPS)


def _rope_table_kernel(pos_ref, inv_ref, sign_ref, cos_ref, sin_ref):
    rows = pos_ref.shape[0]
    half = DIFF_HEAD_DIM // 2
    groups = LANES // half
    ang = pos_ref[...].astype(F32) * inv_ref[...]
    group_of_lane = lax.broadcasted_iota(jnp.int32, ang.shape, 1) // half
    for table, sign, out_ref in ((jnp.cos(ang), None, cos_ref),
                                 (jnp.sin(ang), sign_ref[...], sin_ref)):
        shifted = [table] + [pltpu.roll(table, d * half, 1) for d in range(1, groups)]
        for k in range(groups):
            g = shifted[(groups - k) % groups]
            for j in range(1, groups):
                g = jnp.where(group_of_lane == j, shifted[(j - k) % groups], g)
            out_ref[k * rows:(k + 1) * rows, :] = g if sign is None else g * sign


def _rope_tables(positions, tm):
    m = positions.size
    half = DIFF_HEAD_DIM // 2
    groups = LANES // half
    rows = tm // groups
    inv_freq = ROPE_THETA ** (-jnp.arange(0, DIFF_HEAD_DIM, 2, dtype=F32) / DIFF_HEAD_DIM)
    inv = jnp.tile(inv_freq, groups).reshape(1, LANES)
    sign = jnp.tile(jnp.concatenate([-jnp.ones((half,), F32), jnp.ones((half,), F32)]),
                    LANES // DIFF_HEAD_DIM).reshape(1, LANES)
    pos = positions.reshape(m // tm, groups, rows).transpose(0, 2, 1)
    pos = jnp.repeat(pos, half, axis=2).reshape(m // groups, LANES)
    return pl.pallas_call(
        _rope_table_kernel,
        grid=(m // tm,),
        in_specs=[pl.BlockSpec((rows, LANES), lambda i: (i, 0)),
                  pl.BlockSpec((1, LANES), lambda i: (0, 0)),
                  pl.BlockSpec((1, LANES), lambda i: (0, 0))],
        out_specs=[pl.BlockSpec((tm, LANES), lambda i: (i, 0)),
                   pl.BlockSpec((tm, LANES), lambda i: (i, 0))],
        out_shape=[jax.ShapeDtypeStruct((m, LANES), F32),
                   jax.ShapeDtypeStruct((m, LANES), F32)],
        name="rope_tables",
    )(pos, inv, sign)


def _in_proj_kernel(*refs, rope_tiles, tn):
    if rope_tiles:
        zero_ref, x_ref, g_ref, w_ref, cos_ref, sin_ref, o_ref, h_ref, y_ref = refs
    else:
        zero_ref, x_ref, g_ref, w_ref, o_ref, h_ref, y_ref = refs
    j = pl.program_id(1)

    def emit(acc, rotary):
        if not rotary:
            o_ref[...] = acc.astype(o_ref.dtype)
            return
        cos = cos_ref[...]
        sin = sin_ref[...]
        lane = lax.broadcasted_iota(jnp.int32, cos.shape, 1)
        first_half = (lane % DIFF_HEAD_DIM) < (DIFF_HEAD_DIM // 2)
        half = DIFF_HEAD_DIM // 2
        for c in range(tn // LANES):
            t = acc[:, c * LANES:(c + 1) * LANES]
            partner = jnp.where(first_half, pltpu.roll(t, LANES - half, 1), pltpu.roll(t, half, 1))
            o_ref[:, c * LANES:(c + 1) * LANES] = (t * cos + partner * sin).astype(o_ref.dtype)

    def project():
        return jnp.dot(h_ref[...], w_ref[...], preferred_element_type=F32)

    @pl.when(j == 0)
    def _():
        xf = x_ref[...]
        rs = lax.rsqrt(jnp.mean(xf * xf, axis=-1, keepdims=True) + RMS_EPS)
        z = zero_ref[0]
        for k0 in range(0, h_ref.shape[1], MXU_DEPTH):
            cols = slice(k0, k0 + MXU_DEPTH)
            h_ref[:, cols] = ((x_ref[:, cols] * rs) * g_ref[:, cols]).astype(BF16)
            part = jnp.dot(h_ref[:, cols], w_ref[cols, :], preferred_element_type=F32)
            if k0 == 0:
                y_ref[z] = part
            else:
                y_ref[z] += part
        emit(y_ref[z], rope_tiles > 0)

    if rope_tiles > 1:
        @pl.when((j > 0) & (j < rope_tiles))
        def _():
            emit(project(), True)

    @pl.when(j >= max(rope_tiles, 1))
    def _():
        emit(project(), False)


def _in_proj(x2d, g_all, w_all_bf16, layer, rope, *, tm, tn, rope_cols):
    m, d = x2d.shape
    n = w_all_bf16.shape[2]
    rope_tiles = 0 if rope is None else rope_cols // tn
    in_specs = [pl.BlockSpec(memory_space=pltpu.SMEM),
                pl.BlockSpec((tm, d), lambda i, j: (i, 0)),
                pl.BlockSpec((None, 1, d), lambda i, j: (layer, 0, 0)),
                pl.BlockSpec((None, d, tn), lambda i, j: (layer, 0, j))]
    args = [jnp.zeros((1,), jnp.int32), x2d, g_all, w_all_bf16]
    if rope is not None:
        assert rope_cols % tn == 0
        in_specs += [pl.BlockSpec((tm, LANES), lambda i, j: (i, 0)),
                     pl.BlockSpec((tm, LANES), lambda i, j: (i, 0))]
        args += list(rope)
    return pl.pallas_call(
        functools.partial(_in_proj_kernel, rope_tiles=rope_tiles, tn=tn),
        grid=(m // tm, n // tn),
        in_specs=in_specs,
        out_specs=pl.BlockSpec((tm, tn), lambda i, j: (i, j)),
        out_shape=jax.ShapeDtypeStruct((m, n), BF16),
        scratch_shapes=[pltpu.VMEM((tm, d), BF16), pltpu.VMEM((1, tm, tn), F32)],
        compiler_params=pltpu.CompilerParams(
            dimension_semantics=("arbitrary", "arbitrary"),
            vmem_limit_bytes=VMEM_LIMIT_BYTES),
        name="in_proj_rope" if rope_tiles else "in_proj",
    )(*args)


def _mem_kv_kernel(mem_ref, g_ref, w_ref, o_ref):
    h = (_rms_scale(mem_ref[...]) * g_ref[...]).astype(BF16)
    o_ref[...] = jnp.dot(h, w_ref[...], preferred_element_type=F32).astype(o_ref.dtype)


def _mem_kv(mem2d, g_all, w_all_bf16, layer, *, mem_len):
    m, d = mem2d.shape
    n = w_all_bf16.shape[2]
    return pl.pallas_call(
        _mem_kv_kernel,
        grid=(m // mem_len,),
        in_specs=[pl.BlockSpec((mem_len, d), lambda i: (i, 0)),
                  pl.BlockSpec((None, 1, d), lambda i: (layer, 0, 0)),
                  pl.BlockSpec((None, d, n), lambda i: (layer, 0, 0))],
        out_specs=pl.BlockSpec((mem_len, n), lambda i: (i, 0)),
        out_shape=jax.ShapeDtypeStruct((m, n), BF16),
        compiler_params=pltpu.CompilerParams(
            dimension_semantics=("arbitrary",), vmem_limit_bytes=VMEM_LIMIT_BYTES),
        name="mem_kv",
    )(mem2d, g_all, w_all_bf16)


def _diff_attn_kernel(zero_ref, lam_ref, q_ref, k_ref, v_ref, g_ref, o_ref,
                      vt_ref, sa_ref, sb_ref, mxa_ref, mxb_ref, p_ref, acc1_ref, acc2_ref,
                      *, t, lambda_init):
    nq = q_ref.shape[0] // t
    n_maps = 2
    acc_refs = (acc1_ref, acc2_ref)
    bufs = ((sa_ref, mxa_ref), (sb_ref, mxb_ref))
    z = zero_ref[0]
    half = t // 2

    ones = jnp.ones((BF16_SUBLANES, t), BF16)
    for j in range(nq):
        vt_ref[0:HEAD_DIM, j * t:(j + 1) * t] = (
            v_ref[j * t:(j + 1) * t, :].astype(F32).T.astype(BF16))
        vt_ref[HEAD_DIM:, j * t:(j + 1) * t] = ones

    lp = lam_ref[...]
    lam = (jnp.exp(jnp.sum(lp[0:1] * lp[1:2], axis=1, keepdims=True))
           - jnp.exp(jnp.sum(lp[2:3] * lp[3:4], axis=1, keepdims=True)) + lambda_init)

    def visible(n_keys, n_queries):
        key = lax.broadcasted_iota(jnp.int32, (n_keys, n_queries), 0)
        qry = lax.broadcasted_iota(jnp.int32, (n_keys, n_queries), 1)
        return key <= qry

    def q_maps_of(qi):
        q_t = (q_ref[qi * t:(qi + 1) * t, :].astype(F32) * (LOG2_E * DIFF_HEAD_DIM ** -0.5)).T
        row = lax.broadcasted_iota(jnp.int32, q_t.shape, 0)
        zero = jnp.zeros_like(q_t)
        return (jnp.where(row < DIFF_HEAD_DIM, q_t, zero).astype(BF16),
                jnp.where(row >= DIFF_HEAD_DIM, q_t, zero).astype(BF16))

    def scores(q_maps, j, n, diagonal, buf):
        s_ref, mx_ref = buf
        for mi in range(n_maps):
            if not diagonal:
                s = jnp.dot(k_ref[j * t:(j + n) * t, :], q_maps[mi],
                            preferred_element_type=F32)
                s_ref[z + mi, 0:n * t, :] = s
                mx_ref[mi] = jnp.max(s, axis=0, keepdims=True)
                continue
            s_top = jnp.dot(k_ref[j * t:j * t + half, :], q_maps[mi],
                            preferred_element_type=F32)
            s_top = jnp.where(visible(half, t), s_top, NEG_BIG)
            s_bot = jnp.dot(k_ref[j * t + half:(j + 1) * t, :], q_maps[mi][:, half:],
                            preferred_element_type=F32)
            s_bot = jnp.where(visible(half, half), s_bot, NEG_BIG)
            s_ref[z + mi, 0:half, :] = s_top
            s_ref[z + mi, half:t, half:] = s_bot
            mx_top = jnp.max(s_top, axis=0, keepdims=True)
            mx_ref[mi, :, 0:half] = mx_top[:, :half]
            mx_ref[mi, :, half:] = jnp.maximum(mx_top[:, half:],
                                               jnp.max(s_bot, axis=0, keepdims=True))

    def consume(j, n, diagonal, buf, m_old):
        s_ref, mx_ref = buf
        v_t = vt_ref[:, j * t:(j + n) * t]
        m_out = []
        for mi in range(n_maps):
            acc_ref = acc_refs[mi]
            if m_old is None:
                m_new = mx_ref[mi]
                alpha = None
            else:
                m_new = jnp.maximum(m_old[mi], mx_ref[mi])
                alpha = jnp.exp2(m_old[mi] - m_new)
            if not diagonal:
                p_ref[z + mi, 0:n * t, :] = jnp.exp2(
                    s_ref[z + mi, 0:n * t, :] - m_new).astype(BF16)
                pv = jnp.dot(v_t, p_ref[z + mi, 0:n * t, :], preferred_element_type=F32)
                acc_ref[...] = pv if alpha is None else alpha * acc_ref[...] + pv
            else:
                p_ref[z + mi, 0:half, :] = jnp.exp2(
                    s_ref[z + mi, 0:half, :] - m_new).astype(BF16)
                p_ref[z + mi, half:t, half:] = jnp.exp2(
                    s_ref[z + mi, half:t, half:] - m_new[:, half:]).astype(BF16)
                pv = jnp.dot(v_t[:, :half], p_ref[z + mi, 0:half, :],
                             preferred_element_type=F32)
                pv_late = jnp.dot(v_t[:, half:], p_ref[z + mi, half:t, half:],
                                  preferred_element_type=F32)
                if alpha is None:
                    acc_ref[:, :half] = pv[:, :half]
                    acc_ref[:, half:] = pv[:, half:] + pv_late
                else:
                    acc_ref[:, :half] = alpha[:, :half] * acc_ref[:, :half] + pv[:, :half]
                    acc_ref[:, half:] = (alpha[:, half:] * acc_ref[:, half:]
                                         + pv[:, half:] + pv_late)
            m_out.append(m_new)
        return m_out

    def finalize(qi):
        o1 = acc1_ref[0:HEAD_DIM, :] / acc1_ref[HEAD_DIM:HEAD_DIM + 1, :]
        o2 = acc2_ref[0:HEAD_DIM, :] / acc2_ref[HEAD_DIM:HEAD_DIM + 1, :]
        o = o1 - lam * o2
        o = o * lax.rsqrt(jnp.mean(o * o, axis=0, keepdims=True) + RMS_EPS)
        o_ref[qi * t:(qi + 1) * t, :] = (
            (o.T * g_ref[...]) * (1.0 - lambda_init)).astype(o_ref.dtype)

    steps = []
    for qi in range(nq):
        j = 0
        while j < qi:
            n = KV_STEP if j + KV_STEP <= qi else 1
            steps.append((qi, j, n, False))
            j += n
        steps.append((qi, qi, 1, True))
    q_maps = q_maps_of(0)
    scores(q_maps, steps[0][1], steps[0][2], steps[0][3], bufs[0])
    m_run = None
    for i, (qi, j, n, diagonal) in enumerate(steps):
        if i + 1 < len(steps):
            qn, jn, nn, dn = steps[i + 1]
            if qn != qi:
                q_maps = q_maps_of(qn)
            scores(q_maps, jn, nn, dn, bufs[(i + 1) % 2])
        m_run = consume(j, n, diagonal, bufs[i % 2], m_run)
        if diagonal:
            finalize(qi)
            m_run = None


def _diff_attention(proj, lam_params, subln_g, *, batch, seq, heads, layer_idx, t):
    m = proj.shape[0]
    k_off = heads
    v_off = 2 * heads
    lambda_init = 0.8 - 0.6 * math.exp(-0.3 * layer_idx)
    acc_rows = HEAD_DIM + BF16_SUBLANES
    return pl.pallas_call(
        functools.partial(_diff_attn_kernel, t=t, lambda_init=lambda_init),
        grid=(batch, heads),
        in_specs=[pl.BlockSpec(memory_space=pltpu.SMEM),
                  pl.BlockSpec(lam_params.shape, lambda b, h: (0, 0)),
                  pl.BlockSpec((seq, HEAD_DIM), lambda b, h: (b, h)),
                  pl.BlockSpec((seq, HEAD_DIM), lambda b, h: (b, k_off + h)),
                  pl.BlockSpec((seq, HEAD_DIM), lambda b, h: (b, v_off + h)),
                  pl.BlockSpec((1, HEAD_DIM), lambda b, h: (0, 0))],
        out_specs=pl.BlockSpec((seq, HEAD_DIM), lambda b, h: (b, h)),
        out_shape=jax.ShapeDtypeStruct((m, heads * HEAD_DIM), BF16),
        scratch_shapes=[pltpu.VMEM((acc_rows, seq), BF16),
                        pltpu.VMEM((2, KV_STEP * t, t), F32), pltpu.VMEM((2, KV_STEP * t, t), F32),
                        pltpu.VMEM((2, 1, t), F32), pltpu.VMEM((2, 1, t), F32),
                        pltpu.VMEM((2, KV_STEP * t, t), BF16),
                        pltpu.VMEM((acc_rows, t), F32), pltpu.VMEM((acc_rows, t), F32)],
        compiler_params=pltpu.CompilerParams(
            dimension_semantics=("arbitrary", "arbitrary"),
            vmem_limit_bytes=VMEM_LIMIT_BYTES),
        name="diff_attention",
    )(jnp.zeros((1,), jnp.int32), lam_params, proj, proj, proj, subln_g.reshape(1, HEAD_DIM))


def _silu(g):
    return g * (1.0 / (1.0 + jnp.exp(-g)))


def _mix_out_kernel(*refs, conv, tm, tok_width, tiles_per_seq, mem_len):
    if conv:
        (xin_ref, gb_ref, gc_ref, cw_ref, qm_ref, gate_a_ref, gate_b_ref, kv_ref, wout_ref,
         x_ref, gpost_ref, o_ref, br_ref, y_ref, u_ref) = refs
    else:
        (tok_ref, qm_ref, gate_a_ref, gate_b_ref, kv_ref, wout_ref,
         x_ref, gpost_ref, o_ref, br_ref, y_ref) = refs
    half_gate = gate_a_ref.shape[1]
    chunk = MXU_DEPTH
    pad = CONV_PAD

    def gate_cols(rows, lo, hi):
        if hi <= half_gate:
            return _silu(gate_a_ref[rows, lo:hi])
        assert lo >= half_gate
        return _silu(gate_b_ref[rows, lo - half_gate:hi - half_gate])

    if conv:
        first = (pl.program_id(0) % tiles_per_seq) == 0

        @pl.when(first)
        def _():
            u_ref[0:pad, :] = jnp.zeros((pad, tok_width), F32)

        @pl.when(jnp.logical_not(first))
        def _():
            u_ref[0:pad, :] = u_ref[tm:tm + pad, :]

        u_ref[pad:pad + tm, :] = gc_ref[...].astype(F32) * xin_ref[...].astype(F32)

    def rows_part(r0, r1):
        rows = slice(r0, r1)

        def project(lo, hi, first_chunk):
            part = jnp.dot(br_ref[rows, lo:hi], wout_ref[lo:hi, :], preferred_element_type=F32)
            if first_chunk:
                y_ref[rows, :] = part
            else:
                y_ref[rows, :] += part

        for lo in range(0, tok_width, chunk):
            hi = lo + chunk
            if conv:
                w = cw_ref[:, lo:hi]
                conv_out = (u_ref[pad + r0:pad + r1, lo:hi] * w[2:3]
                            + u_ref[pad - 1 + r0:pad - 1 + r1, lo:hi] * w[1:2]
                            + u_ref[pad - 2 + r0:pad - 2 + r1, lo:hi] * w[0:1])
                tok = (gb_ref[rows, lo:hi].astype(F32) * conv_out).astype(BF16)
            else:
                tok = tok_ref[rows, lo:hi]
            br_ref[rows, lo:hi] = tok * gate_cols(rows, lo, hi)
            project(lo, hi, lo == 0)

        scale = HEAD_DIM ** -0.5
        for h in range(CROSS_HEADS):
            lo = h * HEAD_DIM
            q_h = qm_ref[rows, lo:lo + HEAD_DIM]
            k_h = kv_ref[:, lo:lo + HEAD_DIM]
            v_h = kv_ref[:, CROSS_WIDTH + lo:CROSS_WIDTH + lo + HEAD_DIM]
            s = lax.dot_general(q_h, k_h, (((1,), (1,)), ((), ())),
                                preferred_element_type=F32) * scale
            p = jnp.exp(s - jnp.max(s, axis=-1, keepdims=True))
            o_h = jnp.dot(p.astype(BF16), v_h, preferred_element_type=F32)
            o_h = o_h / jnp.sum(p, axis=-1, keepdims=True)
            col = tok_width + lo
            br_ref[rows, col:col + HEAD_DIM] = (
                o_h.astype(BF16) * gate_cols(rows, col, col + HEAD_DIM))
        for lo in range(tok_width, tok_width + CROSS_WIDTH, chunk):
            project(lo, lo + chunk, False)

        o_ref[rows, :] = x_ref[rows, :] + _rms_scale(y_ref[rows, :]) * gpost_ref[...]

    part_rows = tm // ROW_PARTS
    for r0 in range(0, tm, part_rows):
        rows_part(r0, r0 + part_rows)


def _mix_out(proj, tok, conv_w, kv, w_out_all_bf16, layer, x2d, g_post_all, *, seq, mem_len, tm):
    m, d = x2d.shape
    mix_width = w_out_all_bf16.shape[1]
    tok_width = mix_width - CROSS_WIDTH
    conv = tok is None
    tiles_per_seq = seq // tm
    qm_blk = (3 * tok_width) // CROSS_WIDTH
    half_gate = mix_width // 2
    gate_blk = (3 * tok_width + CROSS_WIDTH) // half_gate
    assert qm_blk * CROSS_WIDTH == 3 * tok_width
    assert gate_blk * half_gate == 3 * tok_width + CROSS_WIDTH

    tail_specs = [pl.BlockSpec((tm, CROSS_WIDTH), lambda i: (i, qm_blk)),
                  pl.BlockSpec((tm, half_gate), lambda i: (i, gate_blk)),
                  pl.BlockSpec((tm, half_gate), lambda i: (i, gate_blk + 1)),
                  pl.BlockSpec((mem_len, 2 * CROSS_WIDTH), lambda i: (i // tiles_per_seq, 0)),
                  pl.BlockSpec((None, mix_width, d), lambda i: (layer, 0, 0),
                               pipeline_mode=pl.Buffered(1)),
                  pl.BlockSpec((tm, d), lambda i: (i, 0)),
                  pl.BlockSpec((None, 1, d), lambda i: (layer, 0, 0))]
    tail_args = [proj, proj, proj, kv, w_out_all_bf16, x2d, g_post_all]
    scratch = [pltpu.VMEM((tm, mix_width), BF16), pltpu.VMEM((tm, d), F32)]
    if conv:
        head_specs = [pl.BlockSpec((tm, tok_width), lambda i: (i, 0)),
                      pl.BlockSpec((tm, tok_width), lambda i: (i, 1)),
                      pl.BlockSpec((tm, tok_width), lambda i: (i, 2)),
                      pl.BlockSpec((CONV_WIDTH, tok_width), lambda i: (0, 0))]
        head_args = [proj, proj, proj, conv_w.T]
        scratch.append(pltpu.VMEM((tm + CONV_PAD, tok_width), F32))
    else:
        head_specs = [pl.BlockSpec((tm, tok_width), lambda i: (i, 0))]
        head_args = [tok]
    return pl.pallas_call(
        functools.partial(_mix_out_kernel, conv=conv, tm=tm, tok_width=tok_width,
                          tiles_per_seq=tiles_per_seq, mem_len=mem_len),
        grid=(m // tm,),
        in_specs=head_specs + tail_specs,
        out_specs=pl.BlockSpec((tm, d), lambda i: (i, 0)),
        out_shape=jax.ShapeDtypeStruct((m, d), F32),
        scratch_shapes=scratch,
        compiler_params=pltpu.CompilerParams(
            dimension_semantics=("arbitrary",), vmem_limit_bytes=VMEM_LIMIT_BYTES),
        name="mix_out_conv" if conv else "mix_out_attn",
    )(*head_args, *tail_args)


def kernel(x, mem, positions, pre_norm, post_norm, mem_norm, w_in, w_kv_mem, w_out,
           conv_w, diff_lambda, diff_subln):
    batch, seq, d = x.shape
    mem_len = mem.shape[1]
    depth = w_in.shape[0]
    tok_width = w_out.shape[1] - CROSS_WIDTH
    heads = tok_width // HEAD_DIM
    x2d = x.reshape(batch * seq, d)
    mem2d = mem.reshape(batch * mem_len, d)
    rope = _rope_tables(positions, tm=IN_PROJ_ROWS) if depth > 1 else None
    w_in_bf16 = w_in.astype(BF16)
    w_kv_bf16 = w_kv_mem.astype(BF16)
    w_out_bf16 = w_out.astype(BF16)
    pre_g = pre_norm.reshape(depth, 1, d)
    post_g = post_norm.reshape(depth, 1, d)
    mem_g = mem_norm.reshape(depth, 1, d)

    for i in range(depth):
        attn_layer = (i % N_MIXERS) == 1
        proj = _in_proj(x2d, pre_g, w_in_bf16, i, rope if attn_layer else None,
                        tm=IN_PROJ_ROWS, tn=IN_PROJ_COLS, rope_cols=2 * tok_width)
        kv = _mem_kv(mem2d, mem_g, w_kv_bf16, i, mem_len=mem_len)
        if attn_layer:
            tok = _diff_attention(proj, diff_lambda[i // N_MIXERS], diff_subln[i // N_MIXERS],
                                  batch=batch, seq=seq, heads=heads, layer_idx=i, t=ATTN_TILE)
            conv = None
        else:
            tok = None
            conv = conv_w[i // N_MIXERS]
        x2d = _mix_out(proj, tok, conv, kv, w_out_bf16, i, x2d, post_g,
                       seq=seq, mem_len=mem_len, tm=MIX_ROWS)
    return x2d.reshape(batch, seq, d)
```

```python
import functools
import math

import jax
import jax.numpy as jnp
from jax import lax
from jax.experimental import pallas as pl
from jax.experimental.pallas import tpu as pltpu

HEAD_DIM = 128
CROSS_HEADS = 4
CROSS_WIDTH = CROSS_HEADS * HEAD_DIM
CONV_WIDTH = 3
DIFF_HEAD_DIM = 64
ROPE_THETA = 10000.0
RMS_EPS = 1e-6
NEG_BIG = -1e30
N_MIXERS = 2
LOG2_E = math.log2(math.e)

LANES = 128
BF16_SUBLANES = 16
CONV_PAD = 8
MXU_DEPTH = 256
ROW_PARTS = 2
VMEM_LIMIT_BYTES = 56 * 1024 * 1024

IN_PROJ_ROWS = 256
IN_PROJ_COLS = 1024
ATTN_TILE = 512
MIX_ROWS = 512

F32 = jnp.float32
BF16 = jnp.bfloat16


def _rms_scale(xf):
    return xf * lax.rsqrt(jnp.mean(xf * xf, axis=-1, keepdims=True) + RMS_EPS)


def _rope_table_kernel(pos_ref, inv_ref, sign_ref, cos_ref, sin_ref):
    rows = pos_ref.shape[0]
    half = DIFF_HEAD_DIM // 2
    groups = LANES // half
    ang = pos_ref[...].astype(F32) * inv_ref[...]
    group_of_lane = lax.broadcasted_iota(jnp.int32, ang.shape, 1) // half
    for table, sign, out_ref in ((jnp.cos(ang), None, cos_ref),
                                 (jnp.sin(ang), sign_ref[...], sin_ref)):
        shifted = [table] + [pltpu.roll(table, d * half, 1) for d in range(1, groups)]
        for k in range(groups):
            g = shifted[(groups - k) % groups]
            for j in range(1, groups):
                g = jnp.where(group_of_lane == j, shifted[(j - k) % groups], g)
            out_ref[k * rows:(k + 1) * rows, :] = g if sign is None else g * sign


def _rope_tables(positions, tm):
    m = positions.size
    half = DIFF_HEAD_DIM // 2
    groups = LANES // half
    rows = tm // groups
    inv_freq = ROPE_THETA ** (-jnp.arange(0, DIFF_HEAD_DIM, 2, dtype=F32) / DIFF_HEAD_DIM)
    inv = jnp.tile(inv_freq, groups).reshape(1, LANES)
    sign = jnp.tile(jnp.concatenate([-jnp.ones((half,), F32), jnp.ones((half,), F32)]),
                    LANES // DIFF_HEAD_DIM).reshape(1, LANES)
    pos = positions.reshape(m // tm, groups, rows).transpose(0, 2, 1)
    pos = jnp.repeat(pos, half, axis=2).reshape(m // groups, LANES)
    return pl.pallas_call(
        _rope_table_kernel,
        grid=(m // tm,),
        in_specs=[pl.BlockSpec((rows, LANES), lambda i: (i, 0)),
                  pl.BlockSpec((1, LANES), lambda i: (0, 0)),
                  pl.BlockSpec((1, LANES), lambda i: (0, 0))],
        out_specs=[pl.BlockSpec((tm, LANES), lambda i: (i, 0)),
                   pl.BlockSpec((tm, LANES), lambda i: (i, 0))],
        out_shape=[jax.ShapeDtypeStruct((m, LANES), F32),
                   jax.ShapeDtypeStruct((m, LANES), F32)],
        name="rope_tables",
    )(pos, inv, sign)


def _in_proj_kernel(*refs, rope_tiles, tn):
    if rope_tiles:
        x_ref, g_ref, w_ref, cos_ref, sin_ref, o_ref, h_ref = refs
    else:
        x_ref, g_ref, w_ref, o_ref, h_ref = refs

    def emit(c, acc, rotary):
        base = c * tn
        if not rotary:
            o_ref[:, base:base + tn] = acc.astype(o_ref.dtype)
            return
        cos = cos_ref[...]
        sin = sin_ref[...]
        lane = lax.broadcasted_iota(jnp.int32, cos.shape, 1)
        first_half = (lane % DIFF_HEAD_DIM) < (DIFF_HEAD_DIM // 2)
        half = DIFF_HEAD_DIM // 2
        for k in range(tn // LANES):
            t = acc[:, k * LANES:(k + 1) * LANES]
            partner = jnp.where(first_half, pltpu.roll(t, LANES - half, 1), pltpu.roll(t, half, 1))
            o_ref[:, base + k * LANES:base + (k + 1) * LANES] = (
                t * cos + partner * sin).astype(o_ref.dtype)

    h_ref[...] = (_rms_scale(x_ref[...]) * g_ref[...]).astype(BF16)
    for c in range(w_ref.shape[1] // tn):
        acc = jnp.dot(h_ref[...], w_ref[:, c * tn:(c + 1) * tn], preferred_element_type=F32)
        emit(c, acc, c < rope_tiles)


def _in_proj(x2d, g_all, w_all_bf16, layer, rope, *, tm, tn, rope_cols):
    m, d = x2d.shape
    n = w_all_bf16.shape[2]
    rope_tiles = 0 if rope is None else rope_cols // tn
    in_specs = [pl.BlockSpec((tm, d), lambda i: (i, 0)),
                pl.BlockSpec((None, 1, d), lambda i: (layer, 0, 0)),
                pl.BlockSpec((None, d, n), lambda i: (layer, 0, 0), pipeline_mode=pl.Buffered(1))]
    args = [x2d, g_all, w_all_bf16]
    if rope is not None:
        assert rope_cols % tn == 0
        in_specs += [pl.BlockSpec((tm, LANES), lambda i: (i, 0)),
                     pl.BlockSpec((tm, LANES), lambda i: (i, 0))]
        args += list(rope)
    return pl.pallas_call(
        functools.partial(_in_proj_kernel, rope_tiles=rope_tiles, tn=tn),
        grid=(m // tm,),
        in_specs=in_specs,
        out_specs=pl.BlockSpec((tm, n), lambda i: (i, 0)),
        out_shape=jax.ShapeDtypeStruct((m, n), BF16),
        scratch_shapes=[pltpu.VMEM((tm, d), BF16)],
        compiler_params=pltpu.CompilerParams(
            dimension_semantics=("arbitrary",),
            vmem_limit_bytes=VMEM_LIMIT_BYTES),
        name="in_proj_rope" if rope_tiles else "in_proj",
    )(*args)


def _mem_kv_kernel(mem_ref, g_ref, w_ref, o_ref):
    h = (_rms_scale(mem_ref[...]) * g_ref[...]).astype(BF16)
    o_ref[...] = jnp.dot(h, w_ref[...], preferred_element_type=F32).astype(o_ref.dtype)


def _mem_kv(mem2d, g_all, w_all_bf16, layer, *, mem_len):
    m, d = mem2d.shape
    n = w_all_bf16.shape[2]
    return pl.pallas_call(
        _mem_kv_kernel,
        grid=(m // mem_len,),
        in_specs=[pl.BlockSpec((mem_len, d), lambda i: (i, 0)),
                  pl.BlockSpec((None, 1, d), lambda i: (layer, 0, 0)),
                  pl.BlockSpec((None, d, n), lambda i: (layer, 0, 0))],
        out_specs=pl.BlockSpec((mem_len, n), lambda i: (i, 0)),
        out_shape=jax.ShapeDtypeStruct((m, n), BF16),
        compiler_params=pltpu.CompilerParams(
            dimension_semantics=("arbitrary",), vmem_limit_bytes=VMEM_LIMIT_BYTES),
        name="mem_kv",
    )(mem2d, g_all, w_all_bf16)


def _diff_attn_kernel(zero_ref, lam_ref, q_ref, k_ref, v_ref, g_ref, o_ref,
                      vt_ref, sa_ref, sb_ref, mxa_ref, mxb_ref, p_ref, acc1_ref, acc2_ref,
                      *, t, lambda_init):
    nq = q_ref.shape[0] // t
    n_maps = 2
    acc_refs = (acc1_ref, acc2_ref)
    bufs = ((sa_ref, mxa_ref), (sb_ref, mxb_ref))
    z = zero_ref[0]
    half = t // 2

    ones = jnp.ones((BF16_SUBLANES, t), BF16)
    for j in range(nq):
        vt_ref[0:HEAD_DIM, j * t:(j + 1) * t] = (
            v_ref[j * t:(j + 1) * t, :].astype(F32).T.astype(BF16))
        vt_ref[HEAD_DIM:, j * t:(j + 1) * t] = ones

    lp = lam_ref[...]
    lam = (jnp.exp(jnp.sum(lp[0:1] * lp[1:2], axis=1, keepdims=True))
           - jnp.exp(jnp.sum(lp[2:3] * lp[3:4], axis=1, keepdims=True)) + lambda_init)

    def visible(n_keys, n_queries):
        key = lax.broadcasted_iota(jnp.int32, (n_keys, n_queries), 0)
        qry = lax.broadcasted_iota(jnp.int32, (n_keys, n_queries), 1)
        return key <= qry

    def q_maps_of(qi):
        q_t = (q_ref[qi * t:(qi + 1) * t, :].astype(F32) * (LOG2_E * DIFF_HEAD_DIM ** -0.5)).T
        row = lax.broadcasted_iota(jnp.int32, q_t.shape, 0)
        zero = jnp.zeros_like(q_t)
        return (jnp.where(row < DIFF_HEAD_DIM, q_t, zero).astype(BF16),
                jnp.where(row >= DIFF_HEAD_DIM, q_t, zero).astype(BF16))

    def scores(q_maps, j, diagonal, buf):
        s_ref, mx_ref = buf
        for mi in range(n_maps):
            if not diagonal:
                s = jnp.dot(k_ref[j * t:(j + 1) * t, :], q_maps[mi],
                            preferred_element_type=F32)
                s_ref[z + mi] = s
                mx_ref[mi] = jnp.max(s, axis=0, keepdims=True)
                continue
            s_top = jnp.dot(k_ref[j * t:j * t + half, :], q_maps[mi],
                            preferred_element_type=F32)
            s_top = jnp.where(visible(half, t), s_top, NEG_BIG)
            s_bot = jnp.dot(k_ref[j * t + half:(j + 1) * t, :], q_maps[mi][:, half:],
                            preferred_element_type=F32)
            s_bot = jnp.where(visible(half, half), s_bot, NEG_BIG)
            s_ref[z + mi, 0:half, :] = s_top
            s_ref[z + mi, half:, half:] = s_bot
            mx_top = jnp.max(s_top, axis=0, keepdims=True)
            mx_ref[mi, :, 0:half] = mx_top[:, :half]
            mx_ref[mi, :, half:] = jnp.maximum(mx_top[:, half:],
                                               jnp.max(s_bot, axis=0, keepdims=True))

    def consume(j, diagonal, buf, m_old):
        s_ref, mx_ref = buf
        v_t = vt_ref[:, j * t:(j + 1) * t]
        m_out = []
        for mi in range(n_maps):
            acc_ref = acc_refs[mi]
            if m_old is None:
                m_new = mx_ref[mi]
                alpha = None
            else:
                m_new = jnp.maximum(m_old[mi], mx_ref[mi])
                alpha = jnp.exp2(m_old[mi] - m_new)
            if not diagonal:
                p_ref[z + mi] = jnp.exp2(s_ref[z + mi] - m_new).astype(BF16)
                pv = jnp.dot(v_t, p_ref[z + mi], preferred_element_type=F32)
                acc_ref[...] = pv if alpha is None else alpha * acc_ref[...] + pv
            else:
                p_ref[z + mi, 0:half, :] = jnp.exp2(
                    s_ref[z + mi, 0:half, :] - m_new).astype(BF16)
                p_ref[z + mi, half:, half:] = jnp.exp2(
                    s_ref[z + mi, half:, half:] - m_new[:, half:]).astype(BF16)
                pv = jnp.dot(v_t[:, :half], p_ref[z + mi, 0:half, :],
                             preferred_element_type=F32)
                pv_late = jnp.dot(v_t[:, half:], p_ref[z + mi, half:, half:],
                                  preferred_element_type=F32)
                if alpha is None:
                    acc_ref[:, :half] = pv[:, :half]
                    acc_ref[:, half:] = pv[:, half:] + pv_late
                else:
                    acc_ref[:, :half] = alpha[:, :half] * acc_ref[:, :half] + pv[:, :half]
                    acc_ref[:, half:] = (alpha[:, half:] * acc_ref[:, half:]
                                         + pv[:, half:] + pv_late)
            m_out.append(m_new)
        return m_out

    def finalize(qi):
        o1 = acc1_ref[0:HEAD_DIM, :] / acc1_ref[HEAD_DIM:HEAD_DIM + 1, :]
        o2 = acc2_ref[0:HEAD_DIM, :] / acc2_ref[HEAD_DIM:HEAD_DIM + 1, :]
        o = o1 - lam * o2
        o = o * lax.rsqrt(jnp.mean(o * o, axis=0, keepdims=True) + RMS_EPS)
        o_ref[qi * t:(qi + 1) * t, :] = (
            (o.T * g_ref[...]) * (1.0 - lambda_init)).astype(o_ref.dtype)

    tiles = [(qi, j) for qi in range(nq) for j in range(qi + 1)]
    q_maps = q_maps_of(0)
    scores(q_maps, 0, True, bufs[0])
    m_run = None
    for i, (qi, j) in enumerate(tiles):
        if i + 1 < len(tiles):
            qn, jn = tiles[i + 1]
            if qn != qi:
                q_maps = q_maps_of(qn)
            scores(q_maps, jn, jn == qn, bufs[(i + 1) % 2])
        m_run = consume(j, j == qi, bufs[i % 2], m_run)
        if j == qi:
            finalize(qi)
            m_run = None


def _diff_attention(proj, lam_params, subln_g, *, batch, seq, heads, layer_idx, t):
    m = proj.shape[0]
    k_off = heads
    v_off = 2 * heads
    lambda_init = 0.8 - 0.6 * math.exp(-0.3 * layer_idx)
    acc_rows = HEAD_DIM + BF16_SUBLANES
    return pl.pallas_call(
        functools.partial(_diff_attn_kernel, t=t, lambda_init=lambda_init),
        grid=(batch, heads),
        in_specs=[pl.BlockSpec(memory_space=pltpu.SMEM),
                  pl.BlockSpec(lam_params.shape, lambda b, h: (0, 0)),
                  pl.BlockSpec((seq, HEAD_DIM), lambda b, h: (b, h)),
                  pl.BlockSpec((seq, HEAD_DIM), lambda b, h: (b, k_off + h)),
                  pl.BlockSpec((seq, HEAD_DIM), lambda b, h: (b, v_off + h)),
                  pl.BlockSpec((1, HEAD_DIM), lambda b, h: (0, 0))],
        out_specs=pl.BlockSpec((seq, HEAD_DIM), lambda b, h: (b, h)),
        out_shape=jax.ShapeDtypeStruct((m, heads * HEAD_DIM), BF16),
        scratch_shapes=[pltpu.VMEM((acc_rows, seq), BF16),
                        pltpu.VMEM((2, t, t), F32), pltpu.VMEM((2, t, t), F32),
                        pltpu.VMEM((2, 1, t), F32), pltpu.VMEM((2, 1, t), F32),
                        pltpu.VMEM((2, t, t), BF16),
                        pltpu.VMEM((acc_rows, t), F32), pltpu.VMEM((acc_rows, t), F32)],
        compiler_params=pltpu.CompilerParams(
            dimension_semantics=("arbitrary", "arbitrary"),
            vmem_limit_bytes=VMEM_LIMIT_BYTES),
        name="diff_attention",
    )(jnp.zeros((1,), jnp.int32), lam_params, proj, proj, proj, subln_g.reshape(1, HEAD_DIM))


def _silu(g):
    return g * (1.0 / (1.0 + jnp.exp(-g)))


def _mix_out_kernel(*refs, conv, tm, tok_width, tiles_per_seq, mem_len):
    if conv:
        (xin_ref, gb_ref, gc_ref, cw_ref, qm_ref, gate_a_ref, gate_b_ref, kv_ref, wout_ref,
         x_ref, gpost_ref, o_ref, br_ref, y_ref, u_ref) = refs
    else:
        (tok_ref, qm_ref, gate_a_ref, gate_b_ref, kv_ref, wout_ref,
         x_ref, gpost_ref, o_ref, br_ref, y_ref) = refs
    half_gate = gate_a_ref.shape[1]
    chunk = MXU_DEPTH
    pad = CONV_PAD

    def gate_cols(rows, lo, hi):
        if hi <= half_gate:
            return _silu(gate_a_ref[rows, lo:hi])
        assert lo >= half_gate
        return _silu(gate_b_ref[rows, lo - half_gate:hi - half_gate])

    if conv:
        first = (pl.program_id(0) % tiles_per_seq) == 0

        @pl.when(first)
        def _():
            u_ref[0:pad, :] = jnp.zeros((pad, tok_width), F32)

        @pl.when(jnp.logical_not(first))
        def _():
            u_ref[0:pad, :] = u_ref[tm:tm + pad, :]

        u_ref[pad:pad + tm, :] = gc_ref[...].astype(F32) * xin_ref[...].astype(F32)

    def rows_part(r0, r1):
        rows = slice(r0, r1)

        def project(lo, hi, first_chunk):
            part = jnp.dot(br_ref[rows, lo:hi], wout_ref[lo:hi, :], preferred_element_type=F32)
            if first_chunk:
                y_ref[rows, :] = part
            else:
                y_ref[rows, :] += part

        for lo in range(0, tok_width, chunk):
            hi = lo + chunk
            if conv:
                w = cw_ref[:, lo:hi]
                conv_out = (u_ref[pad + r0:pad + r1, lo:hi] * w[2:3]
                            + u_ref[pad - 1 + r0:pad - 1 + r1, lo:hi] * w[1:2]
                            + u_ref[pad - 2 + r0:pad - 2 + r1, lo:hi] * w[0:1])
                tok = (gb_ref[rows, lo:hi].astype(F32) * conv_out).astype(BF16)
            else:
                tok = tok_ref[rows, lo:hi]
            br_ref[rows, lo:hi] = tok * gate_cols(rows, lo, hi)
            project(lo, hi, lo == 0)

        scale = HEAD_DIM ** -0.5
        for h in range(CROSS_HEADS):
            lo = h * HEAD_DIM
            q_h = qm_ref[rows, lo:lo + HEAD_DIM]
            k_h = kv_ref[:, lo:lo + HEAD_DIM]
            v_h = kv_ref[:, CROSS_WIDTH + lo:CROSS_WIDTH + lo + HEAD_DIM]
            s = lax.dot_general(q_h, k_h, (((1,), (1,)), ((), ())),
                                preferred_element_type=F32) * scale
            p = jnp.exp(s - jnp.max(s, axis=-1, keepdims=True))
            o_h = jnp.dot(p.astype(BF16), v_h, preferred_element_type=F32)
            o_h = o_h / jnp.sum(p, axis=-1, keepdims=True)
            col = tok_width + lo
            br_ref[rows, col:col + HEAD_DIM] = (
                o_h.astype(BF16) * gate_cols(rows, col, col + HEAD_DIM))
        for lo in range(tok_width, tok_width + CROSS_WIDTH, chunk):
            project(lo, lo + chunk, False)

        o_ref[rows, :] = x_ref[rows, :] + _rms_scale(y_ref[rows, :]) * gpost_ref[...]

    part_rows = tm // ROW_PARTS
    for r0 in range(0, tm, part_rows):
        rows_part(r0, r0 + part_rows)


def _mix_out(proj, tok, conv_w, kv, w_out_all_bf16, layer, x2d, g_post_all, *, seq, mem_len, tm):
    m, d = x2d.shape
    mix_width = w_out_all_bf16.shape[1]
    tok_width = mix_width - CROSS_WIDTH
    conv = tok is None
    tiles_per_seq = seq // tm
    qm_blk = (3 * tok_width) // CROSS_WIDTH
    half_gate = mix_width // 2
    gate_blk = (3 * tok_width + CROSS_WIDTH) // half_gate
    assert qm_blk * CROSS_WIDTH == 3 * tok_width
    assert gate_blk * half_gate == 3 * tok_width + CROSS_WIDTH

    tail_specs = [pl.BlockSpec((tm, CROSS_WIDTH), lambda i: (i, qm_blk)),
                  pl.BlockSpec((tm, half_gate), lambda i: (i, gate_blk)),
                  pl.BlockSpec((tm, half_gate), lambda i: (i, gate_blk + 1)),
                  pl.BlockSpec((mem_len, 2 * CROSS_WIDTH), lambda i: (i // tiles_per_seq, 0)),
                  pl.BlockSpec((None, mix_width, d), lambda i: (layer, 0, 0),
                               pipeline_mode=pl.Buffered(1)),
                  pl.BlockSpec((tm, d), lambda i: (i, 0)),
                  pl.BlockSpec((None, 1, d), lambda i: (layer, 0, 0))]
    tail_args = [proj, proj, proj, kv, w_out_all_bf16, x2d, g_post_all]
    scratch = [pltpu.VMEM((tm, mix_width), BF16), pltpu.VMEM((tm, d), F32)]
    if conv:
        head_specs = [pl.BlockSpec((tm, tok_width), lambda i: (i, 0)),
                      pl.BlockSpec((tm, tok_width), lambda i: (i, 1)),
                      pl.BlockSpec((tm, tok_width), lambda i: (i, 2)),
                      pl.BlockSpec((CONV_WIDTH, tok_width), lambda i: (0, 0))]
        head_args = [proj, proj, proj, conv_w.T]
        scratch.append(pltpu.VMEM((tm + CONV_PAD, tok_width), F32))
    else:
        head_specs = [pl.BlockSpec((tm, tok_width), lambda i: (i, 0))]
        head_args = [tok]
    return pl.pallas_call(
        functools.partial(_mix_out_kernel, conv=conv, tm=tm, tok_width=tok_width,
                          tiles_per_seq=tiles_per_seq, mem_len=mem_len),
        grid=(m // tm,),
        in_specs=head_specs + tail_specs,
        out_specs=pl.BlockSpec((tm, d), lambda i: (i, 0)),
        out_shape=jax.ShapeDtypeStruct((m, d), F32),
        scratch_shapes=scratch,
        compiler_params=pltpu.CompilerParams(
            dimension_semantics=("arbitrary",), vmem_limit_bytes=VMEM_LIMIT_BYTES),
        name="mix_out_conv" if conv else "mix_out_attn",
    )(*head_args, *tail_args)


def kernel(x, mem, positions, pre_norm, post_norm, mem_norm, w_in, w_kv_mem, w_out,
           conv_w, diff_lambda, diff_subln):
    batch, seq, d = x.shape
    mem_len = mem.shape[1]
    depth = w_in.shape[0]
    tok_width = w_out.shape[1] - CROSS_WIDTH
    heads = tok_width // HEAD_DIM
    x2d = x.reshape(batch * seq, d)
    mem2d = mem.reshape(batch * mem_len, d)
    rope = _rope_tables(positions, tm=IN_PROJ_ROWS) if depth > 1 else None
    w_in_bf16 = w_in.astype(BF16)
    w_kv_bf16 = w_kv_mem.astype(BF16)
    w_out_bf16 = w_out.astype(BF16)
    pre_g = pre_norm.reshape(depth, 1, d)
    post_g = post_norm.reshape(depth, 1, d)
    mem_g = mem_norm.reshape(depth, 1, d)

    for i in range(depth):
        attn_layer = (i % N_MIXERS) == 1
        proj = _in_proj(x2d, pre_g, w_in_bf16, i, rope if attn_layer else None,
                        tm=IN_PROJ_ROWS, tn=IN_PROJ_COLS, rope_cols=2 * tok_width)
        kv = _mem_kv(mem2d, mem_g, w_kv_bf16, i, mem_len=mem_len)
        if attn_layer:
            tok = _diff_attention(proj, diff_lambda[i // N_MIXERS], diff_subln[i // N_MIXERS],
                                  batch=batch, seq=seq, heads=heads, layer_idx=i, t=ATTN_TILE)
            conv = None
        else:
            tok = None
            conv = conv_w[i // N_MIXERS]
        x2d = _mix_out(proj, tok, conv, kv, w_out_bf16, i, x2d, post_g,
                       seq=seq, mem_len=mem_len, tm=MIX_ROWS)
    return x2d.reshape(batch, seq, d)
```

```python
import functools
import math

import jax
import jax.numpy as jnp
from jax import lax
from jax.experimental import pallas as pl
from jax.experimental.pallas import tpu as pltpu

HEAD_DIM = 128
CROSS_HEADS = 4
CROSS_WIDTH = CROSS_HEADS * HEAD_DIM
CONV_WIDTH = 3
DIFF_HEAD_DIM = 64
ROPE_THETA = 10000.0
RMS_EPS = 1e-6
NEG_BIG = -1e30
N_MIXERS = 2
LOG2_E = math.log2(math.e)

LANES = 128
BF16_SUBLANES = 16
CONV_PAD = 8
MXU_DEPTH = 256
ROW_PARTS = 2
SCORE_BUFFERS = 2
VMEM_LIMIT_BYTES = 56 * 1024 * 1024

IN_PROJ_ROWS = 256
IN_PROJ_COLS = 1024
ROPE_TABLE_ROWS = 1024
ATTN_TILE = 512
MIX_ROWS = 512

F32 = jnp.float32
BF16 = jnp.bfloat16


def _rms_scale(xf):
    return xf * lax.rsqrt(jnp.mean(xf * xf, axis=-1, keepdims=True) + RMS_EPS)


def _rope_table_kernel(pos_ref, inv_ref, sign_ref, cos_ref, sin_ref):
    rows = pos_ref.shape[0]
    half = DIFF_HEAD_DIM // 2
    groups = LANES // half
    ang = pos_ref[...].astype(F32) * inv_ref[...]
    group_of_lane = lax.broadcasted_iota(jnp.int32, ang.shape, 1) // half
    for table, sign, out_ref in ((jnp.cos(ang), None, cos_ref),
                                 (jnp.sin(ang), sign_ref[...], sin_ref)):
        shifted = [table] + [pltpu.roll(table, d * half, 1) for d in range(1, groups)]
        for k in range(groups):
            g = shifted[(groups - k) % groups]
            for j in range(1, groups):
                g = jnp.where(group_of_lane == j, shifted[(j - k) % groups], g)
            out_ref[k * rows:(k + 1) * rows, :] = g if sign is None else g * sign


def _rope_tables(positions, tm):
    m = positions.size
    half = DIFF_HEAD_DIM // 2
    groups = LANES // half
    rows = tm // groups
    inv_freq = ROPE_THETA ** (-jnp.arange(0, DIFF_HEAD_DIM, 2, dtype=F32) / DIFF_HEAD_DIM)
    inv = jnp.tile(inv_freq, groups).reshape(1, LANES)
    sign = jnp.tile(jnp.concatenate([-jnp.ones((half,), F32), jnp.ones((half,), F32)]),
                    LANES // DIFF_HEAD_DIM).reshape(1, LANES)
    pos = positions.reshape(m // tm, groups, rows).transpose(0, 2, 1)
    pos = jnp.repeat(pos, half, axis=2).reshape(m // groups, LANES)
    return pl.pallas_call(
        _rope_table_kernel,
        grid=(m // tm,),
        in_specs=[pl.BlockSpec((rows, LANES), lambda i: (i, 0)),
                  pl.BlockSpec((1, LANES), lambda i: (0, 0)),
                  pl.BlockSpec((1, LANES), lambda i: (0, 0))],
        out_specs=[pl.BlockSpec((tm, LANES), lambda i: (i, 0)),
                   pl.BlockSpec((tm, LANES), lambda i: (i, 0))],
        out_shape=[jax.ShapeDtypeStruct((m, LANES), F32),
                   jax.ShapeDtypeStruct((m, LANES), F32)],
        name="rope_tables",
    )(pos, inv, sign)


def _in_proj_kernel(*refs, rope_tiles, tn):
    if rope_tiles:
        x_ref, g_ref, w_ref, cos_ref, sin_ref, o_ref, h_ref = refs
    else:
        x_ref, g_ref, w_ref, o_ref, h_ref = refs

    def emit(c, acc, rotary):
        base = c * tn
        if not rotary:
            o_ref[:, base:base + tn] = acc.astype(o_ref.dtype)
            return
        cos = cos_ref[...]
        sin = sin_ref[...]
        lane = lax.broadcasted_iota(jnp.int32, cos.shape, 1)
        first_half = (lane % DIFF_HEAD_DIM) < (DIFF_HEAD_DIM // 2)
        half = DIFF_HEAD_DIM // 2
        for k in range(tn // LANES):
            t = acc[:, k * LANES:(k + 1) * LANES]
            partner = jnp.where(first_half, pltpu.roll(t, LANES - half, 1), pltpu.roll(t, half, 1))
            o_ref[:, base + k * LANES:base + (k + 1) * LANES] = (
                t * cos + partner * sin).astype(o_ref.dtype)

    h_ref[...] = (_rms_scale(x_ref[...]) * g_ref[...]).astype(BF16)
    for c in range(w_ref.shape[1] // tn):
        acc = jnp.dot(h_ref[...], w_ref[:, c * tn:(c + 1) * tn], preferred_element_type=F32)
        emit(c, acc, c < rope_tiles)


def _in_proj(x2d, g_all, w_all_bf16, layer, rope, *, tm, tn, rope_cols):
    m, d = x2d.shape
    n = w_all_bf16.shape[2]
    rope_tiles = 0 if rope is None else rope_cols // tn
    in_specs = [pl.BlockSpec((tm, d), lambda i: (i, 0)),
                pl.BlockSpec((None, 1, d), lambda i: (layer, 0, 0)),
                pl.BlockSpec((None, d, n), lambda i: (layer, 0, 0), pipeline_mode=pl.Buffered(1))]
    args = [x2d, g_all, w_all_bf16]
    if rope is not None:
        assert rope_cols % tn == 0
        in_specs += [pl.BlockSpec((tm, LANES), lambda i: (i, 0)),
                     pl.BlockSpec((tm, LANES), lambda i: (i, 0))]
        args += list(rope)
    return pl.pallas_call(
        functools.partial(_in_proj_kernel, rope_tiles=rope_tiles, tn=tn),
        grid=(m // tm,),
        in_specs=in_specs,
        out_specs=pl.BlockSpec((tm, n), lambda i: (i, 0)),
        out_shape=jax.ShapeDtypeStruct((m, n), BF16),
        scratch_shapes=[pltpu.VMEM((tm, d), BF16)],
        compiler_params=pltpu.CompilerParams(
            dimension_semantics=("arbitrary",),
            vmem_limit_bytes=VMEM_LIMIT_BYTES),
        name="in_proj_rope" if rope_tiles else "in_proj",
    )(*args)


def _mem_kv_kernel(mem_ref, g_ref, w_ref, o_ref):
    h = (_rms_scale(mem_ref[...]) * g_ref[...]).astype(BF16)
    o_ref[...] = jnp.dot(h, w_ref[...], preferred_element_type=F32).astype(o_ref.dtype)


def _mem_kv(mem2d, g_all, w_all_bf16, layer, *, mem_len):
    m, d = mem2d.shape
    n = w_all_bf16.shape[2]
    return pl.pallas_call(
        _mem_kv_kernel,
        grid=(m // mem_len,),
        in_specs=[pl.BlockSpec((mem_len, d), lambda i: (i, 0)),
                  pl.BlockSpec((None, 1, d), lambda i: (layer, 0, 0)),
                  pl.BlockSpec((None, d, n), lambda i: (layer, 0, 0))],
        out_specs=pl.BlockSpec((mem_len, n), lambda i: (i, 0)),
        out_shape=jax.ShapeDtypeStruct((m, n), BF16),
        compiler_params=pltpu.CompilerParams(
            dimension_semantics=("arbitrary",), vmem_limit_bytes=VMEM_LIMIT_BYTES),
        name="mem_kv",
    )(mem2d, g_all, w_all_bf16)


def _diff_attn_kernel(zero_ref, lam_ref, q_ref, k_ref, v_ref, g_ref, o_ref, vt_ref, *scratch,
                      t, lambda_init):
    s_refs = scratch[:SCORE_BUFFERS]
    mx_refs = scratch[SCORE_BUFFERS:2 * SCORE_BUFFERS]
    p_ref, acc1_ref, acc2_ref = scratch[2 * SCORE_BUFFERS:]
    nq = q_ref.shape[0] // t
    n_maps = 2
    acc_refs = (acc1_ref, acc2_ref)
    bufs = tuple(zip(s_refs, mx_refs))
    z = zero_ref[0]
    half = t // 2

    ones = jnp.ones((BF16_SUBLANES, t), BF16)
    for j in range(nq):
        vt_ref[0:HEAD_DIM, j * t:(j + 1) * t] = (
            v_ref[j * t:(j + 1) * t, :].astype(F32).T.astype(BF16))
        vt_ref[HEAD_DIM:, j * t:(j + 1) * t] = ones

    lp = lam_ref[...]
    lam = (jnp.exp(jnp.sum(lp[0:1] * lp[1:2], axis=1, keepdims=True))
           - jnp.exp(jnp.sum(lp[2:3] * lp[3:4], axis=1, keepdims=True)) + lambda_init)

    def visible(n_keys, n_queries):
        key = lax.broadcasted_iota(jnp.int32, (n_keys, n_queries), 0)
        qry = lax.broadcasted_iota(jnp.int32, (n_keys, n_queries), 1)
        return key <= qry

    def q_maps_of(qi):
        q_t = (q_ref[qi * t:(qi + 1) * t, :].astype(F32) * (LOG2_E * DIFF_HEAD_DIM ** -0.5)).T
        row = lax.broadcasted_iota(jnp.int32, q_t.shape, 0)
        zero = jnp.zeros_like(q_t)
        return (jnp.where(row < DIFF_HEAD_DIM, q_t, zero).astype(BF16),
                jnp.where(row >= DIFF_HEAD_DIM, q_t, zero).astype(BF16))

    def scores(q_maps, j, diagonal, buf):
        s_ref, mx_ref = buf
        for mi in range(n_maps):
            if not diagonal:
                s = jnp.dot(k_ref[j * t:(j + 1) * t, :], q_maps[mi],
                            preferred_element_type=F32)
                s_ref[z + mi] = s
                mx_ref[mi] = jnp.max(s, axis=0, keepdims=True)
                continue
            s_top = jnp.dot(k_ref[j * t:j * t + half, :], q_maps[mi],
                            preferred_element_type=F32)
            s_top = jnp.where(visible(half, t), s_top, NEG_BIG)
            s_bot = jnp.dot(k_ref[j * t + half:(j + 1) * t, :], q_maps[mi][:, half:],
                            preferred_element_type=F32)
            s_bot = jnp.where(visible(half, half), s_bot, NEG_BIG)
            s_ref[z + mi, 0:half, :] = s_top
            s_ref[z + mi, half:, half:] = s_bot
            mx_top = jnp.max(s_top, axis=0, keepdims=True)
            mx_ref[mi, :, 0:half] = mx_top[:, :half]
            mx_ref[mi, :, half:] = jnp.maximum(mx_top[:, half:],
                                               jnp.max(s_bot, axis=0, keepdims=True))

    def consume(j, diagonal, buf, m_old):
        s_ref, mx_ref = buf
        v_t = vt_ref[:, j * t:(j + 1) * t]
        m_out = []
        for mi in range(n_maps):
            acc_ref = acc_refs[mi]
            if m_old is None:
                m_new = mx_ref[mi]
                alpha = None
            else:
                m_new = jnp.maximum(m_old[mi], mx_ref[mi])
                alpha = jnp.exp2(m_old[mi] - m_new)
            if not diagonal:
                p_ref[z + mi] = jnp.exp2(s_ref[z + mi] - m_new).astype(BF16)
                pv = jnp.dot(v_t, p_ref[z + mi], preferred_element_type=F32)
                acc_ref[...] = pv if alpha is None else alpha * acc_ref[...] + pv
            else:
                p_ref[z + mi, 0:half, :] = jnp.exp2(
                    s_ref[z + mi, 0:half, :] - m_new).astype(BF16)
                p_ref[z + mi, half:, half:] = jnp.exp2(
                    s_ref[z + mi, half:, half:] - m_new[:, half:]).astype(BF16)
                pv = jnp.dot(v_t[:, :half], p_ref[z + mi, 0:half, :],
                             preferred_element_type=F32)
                pv_late = jnp.dot(v_t[:, half:], p_ref[z + mi, half:, half:],
                                  preferred_element_type=F32)
                if alpha is None:
                    acc_ref[:, :half] = pv[:, :half]
                    acc_ref[:, half:] = pv[:, half:] + pv_late
                else:
                    acc_ref[:, :half] = alpha[:, :half] * acc_ref[:, :half] + pv[:, :half]
                    acc_ref[:, half:] = (alpha[:, half:] * acc_ref[:, half:]
                                         + pv[:, half:] + pv_late)
            m_out.append(m_new)
        return m_out

    def finalize(qi):
        o1 = acc1_ref[0:HEAD_DIM, :] / acc1_ref[HEAD_DIM:HEAD_DIM + 1, :]
        o2 = acc2_ref[0:HEAD_DIM, :] / acc2_ref[HEAD_DIM:HEAD_DIM + 1, :]
        o = o1 - lam * o2
        o = o * lax.rsqrt(jnp.mean(o * o, axis=0, keepdims=True) + RMS_EPS)
        o_ref[qi * t:(qi + 1) * t, :] = (
            (o.T * g_ref[...]) * (1.0 - lambda_init)).astype(o_ref.dtype)

    tiles = [(qi, j) for qi in range(nq) for j in range(qi + 1)]
    ahead = SCORE_BUFFERS - 1
    q_maps_row = {}

    def issue_scores(i):
        qn, jn = tiles[i]
        if qn not in q_maps_row:
            q_maps_row[qn] = q_maps_of(qn)
        scores(q_maps_row[qn], jn, jn == qn, bufs[i % SCORE_BUFFERS])

    for i in range(min(ahead, len(tiles))):
        issue_scores(i)
    m_run = None
    for i, (qi, j) in enumerate(tiles):
        if i + ahead < len(tiles):
            issue_scores(i + ahead)
        m_run = consume(j, j == qi, bufs[i % SCORE_BUFFERS], m_run)
        if j == qi:
            finalize(qi)
            m_run = None


def _diff_attention(proj, lam_params, subln_g, *, batch, seq, heads, layer_idx, t):
    m = proj.shape[0]
    k_off = heads
    v_off = 2 * heads
    lambda_init = 0.8 - 0.6 * math.exp(-0.3 * layer_idx)
    acc_rows = HEAD_DIM + BF16_SUBLANES
    return pl.pallas_call(
        functools.partial(_diff_attn_kernel, t=t, lambda_init=lambda_init),
        grid=(batch, heads),
        in_specs=[pl.BlockSpec(memory_space=pltpu.SMEM),
                  pl.BlockSpec(lam_params.shape, lambda b, h: (0, 0)),
                  pl.BlockSpec((seq, HEAD_DIM), lambda b, h: (b, h)),
                  pl.BlockSpec((seq, HEAD_DIM), lambda b, h: (b, k_off + h)),
                  pl.BlockSpec((seq, HEAD_DIM), lambda b, h: (b, v_off + h)),
                  pl.BlockSpec((1, HEAD_DIM), lambda b, h: (0, 0))],
        out_specs=pl.BlockSpec((seq, HEAD_DIM), lambda b, h: (b, h)),
        out_shape=jax.ShapeDtypeStruct((m, heads * HEAD_DIM), BF16),
        scratch_shapes=([pltpu.VMEM((acc_rows, seq), BF16)]
                        + [pltpu.VMEM((2, t, t), F32)] * SCORE_BUFFERS
                        + [pltpu.VMEM((2, 1, t), F32)] * SCORE_BUFFERS
                        + [pltpu.VMEM((2, t, t), BF16),
                           pltpu.VMEM((acc_rows, t), F32), pltpu.VMEM((acc_rows, t), F32)]),
        compiler_params=pltpu.CompilerParams(
            dimension_semantics=("arbitrary", "arbitrary"),
            vmem_limit_bytes=VMEM_LIMIT_BYTES),
        name="diff_attention",
    )(jnp.zeros((1,), jnp.int32), lam_params, proj, proj, proj, subln_g.reshape(1, HEAD_DIM))


def _silu(g):
    return g * (1.0 / (1.0 + jnp.exp(-g)))


def _mix_out_kernel(*refs, conv, tm, tok_width, tiles_per_seq, mem_len):
    if conv:
        (xin_ref, gb_ref, gc_ref, cw_ref, qm_ref, gate_a_ref, gate_b_ref, kv_ref, wout_ref,
         x_ref, gpost_ref, o_ref, br_ref, y_ref, u_ref) = refs
    else:
        (tok_ref, qm_ref, gate_a_ref, gate_b_ref, kv_ref, wout_ref,
         x_ref, gpost_ref, o_ref, br_ref, y_ref) = refs
    half_gate = gate_a_ref.shape[1]
    chunk = MXU_DEPTH
    pad = CONV_PAD

    def gate_cols(rows, lo, hi):
        if hi <= half_gate:
            return _silu(gate_a_ref[rows, lo:hi])
        assert lo >= half_gate
        return _silu(gate_b_ref[rows, lo - half_gate:hi - half_gate])

    if conv:
        first = (pl.program_id(0) % tiles_per_seq) == 0

        @pl.when(first)
        def _():
            u_ref[0:pad, :] = jnp.zeros((pad, tok_width), F32)

        @pl.when(jnp.logical_not(first))
        def _():
            u_ref[0:pad, :] = u_ref[tm:tm + pad, :]

        u_ref[pad:pad + tm, :] = gc_ref[...].astype(F32) * xin_ref[...].astype(F32)

    def rows_part(r0, r1):
        rows = slice(r0, r1)

        def project(lo, hi, first_chunk):
            part = jnp.dot(br_ref[rows, lo:hi], wout_ref[lo:hi, :], preferred_element_type=F32)
            if first_chunk:
                y_ref[rows, :] = part
            else:
                y_ref[rows, :] += part

        for lo in range(0, tok_width, chunk):
            hi = lo + chunk
            if conv:
                w = cw_ref[:, lo:hi]
                conv_out = (u_ref[pad + r0:pad + r1, lo:hi] * w[2:3]
                            + u_ref[pad - 1 + r0:pad - 1 + r1, lo:hi] * w[1:2]
                            + u_ref[pad - 2 + r0:pad - 2 + r1, lo:hi] * w[0:1])
                tok = (gb_ref[rows, lo:hi].astype(F32) * conv_out).astype(BF16)
            else:
                tok = tok_ref[rows, lo:hi]
            br_ref[rows, lo:hi] = tok * gate_cols(rows, lo, hi)
            project(lo, hi, lo == 0)

        scale = HEAD_DIM ** -0.5
        for h in range(CROSS_HEADS):
            lo = h * HEAD_DIM
            q_h = qm_ref[rows, lo:lo + HEAD_DIM]
            k_h = kv_ref[:, lo:lo + HEAD_DIM]
            v_h = kv_ref[:, CROSS_WIDTH + lo:CROSS_WIDTH + lo + HEAD_DIM]
            s = lax.dot_general(q_h, k_h, (((1,), (1,)), ((), ())),
                                preferred_element_type=F32) * scale
            p = jnp.exp(s - jnp.max(s, axis=-1, keepdims=True))
            o_h = jnp.dot(p.astype(BF16), v_h, preferred_element_type=F32)
            o_h = o_h / jnp.sum(p, axis=-1, keepdims=True)
            col = tok_width + lo
            br_ref[rows, col:col + HEAD_DIM] = (
                o_h.astype(BF16) * gate_cols(rows, col, col + HEAD_DIM))
        for lo in range(tok_width, tok_width + CROSS_WIDTH, chunk):
            project(lo, lo + chunk, False)

        o_ref[rows, :] = x_ref[rows, :] + _rms_scale(y_ref[rows, :]) * gpost_ref[...]

    part_rows = tm // ROW_PARTS
    for r0 in range(0, tm, part_rows):
        rows_part(r0, r0 + part_rows)


def _mix_out(proj, tok, conv_w, kv, w_out_all_bf16, layer, x2d, g_post_all, *, seq, mem_len, tm):
    m, d = x2d.shape
    mix_width = w_out_all_bf16.shape[1]
    tok_width = mix_width - CROSS_WIDTH
    conv = tok is None
    tiles_per_seq = seq // tm
    qm_blk = (3 * tok_width) // CROSS_WIDTH
    half_gate = mix_width // 2
    gate_blk = (3 * tok_width + CROSS_WIDTH) // half_gate
    assert qm_blk * CROSS_WIDTH == 3 * tok_width
    assert gate_blk * half_gate == 3 * tok_width + CROSS_WIDTH

    tail_specs = [pl.BlockSpec((tm, CROSS_WIDTH), lambda i: (i, qm_blk)),
                  pl.BlockSpec((tm, half_gate), lambda i: (i, gate_blk)),
                  pl.BlockSpec((tm, half_gate), lambda i: (i, gate_blk + 1)),
                  pl.BlockSpec((mem_len, 2 * CROSS_WIDTH), lambda i: (i // tiles_per_seq, 0)),
                  pl.BlockSpec((None, mix_width, d), lambda i: (layer, 0, 0),
                               pipeline_mode=pl.Buffered(1)),
                  pl.BlockSpec((tm, d), lambda i: (i, 0)),
                  pl.BlockSpec((None, 1, d), lambda i: (layer, 0, 0))]
    tail_args = [proj, proj, proj, kv, w_out_all_bf16, x2d, g_post_all]
    scratch = [pltpu.VMEM((tm, mix_width), BF16), pltpu.VMEM((tm, d), F32)]
    if conv:
        head_specs = [pl.BlockSpec((tm, tok_width), lambda i: (i, 0)),
                      pl.BlockSpec((tm, tok_width), lambda i: (i, 1)),
                      pl.BlockSpec((tm, tok_width), lambda i: (i, 2)),
                      pl.BlockSpec((CONV_WIDTH, tok_width), lambda i: (0, 0))]
        head_args = [proj, proj, proj, conv_w.T]
        scratch.append(pltpu.VMEM((tm + CONV_PAD, tok_width), F32))
    else:
        head_specs = [pl.BlockSpec((tm, tok_width), lambda i: (i, 0))]
        head_args = [tok]
    return pl.pallas_call(
        functools.partial(_mix_out_kernel, conv=conv, tm=tm, tok_width=tok_width,
                          tiles_per_seq=tiles_per_seq, mem_len=mem_len),
        grid=(m // tm,),
        in_specs=head_specs + tail_specs,
        out_specs=pl.BlockSpec((tm, d), lambda i: (i, 0)),
        out_shape=jax.ShapeDtypeStruct((m, d), F32),
        scratch_shapes=scratch,
        compiler_params=pltpu.CompilerParams(
            dimension_semantics=("arbitrary",), vmem_limit_bytes=VMEM_LIMIT_BYTES),
        name="mix_out_conv" if conv else "mix_out_attn",
    )(*head_args, *tail_args)


def kernel(x, mem, positions, pre_norm, post_norm, mem_norm, w_in, w_kv_mem, w_out,
           conv_w, diff_lambda, diff_subln):
    batch, seq, d = x.shape
    mem_len = mem.shape[1]
    depth = w_in.shape[0]
    tok_width = w_out.shape[1] - CROSS_WIDTH
    heads = tok_width // HEAD_DIM
    x2d = x.reshape(batch * seq, d)
    mem2d = mem.reshape(batch * mem_len, d)
    rope = _rope_tables(positions, tm=ROPE_TABLE_ROWS) if depth > 1 else None
    w_in_bf16 = w_in.astype(BF16)
    w_kv_bf16 = w_kv_mem.astype(BF16)
    w_out_bf16 = w_out.astype(BF16)
    pre_g = pre_norm.reshape(depth, 1, d)
    post_g = post_norm.reshape(depth, 1, d)
    mem_g = mem_norm.reshape(depth, 1, d)

    for i in range(depth):
        attn_layer = (i % N_MIXERS) == 1
        proj = _in_proj(x2d, pre_g, w_in_bf16, i, rope if attn_layer else None,
                        tm=IN_PROJ_ROWS, tn=IN_PROJ_COLS, rope_cols=2 * tok_width)
        kv = _mem_kv(mem2d, mem_g, w_kv_bf16, i, mem_len=mem_len)
        if attn_layer:
            tok = _diff_attention(proj, diff_lambda[i // N_MIXERS], diff_subln[i // N_MIXERS],
                                  batch=batch, seq=seq, heads=heads, layer_idx=i, t=ATTN_TILE)
            conv = None
        else:
            tok = None
            conv = conv_w[i // N_MIXERS]
        x2d = _mix_out(proj, tok, conv, kv, w_out_bf16, i, x2d, post_g,
                       seq=seq, mem_len=mem_len, tm=MIX_ROWS)
    return x2d.reshape(batch, seq, d)
```

```python
import functools
import math

import jax
import jax.numpy as jnp
from jax import lax
from jax.experimental import pallas as pl
from jax.experimental.pallas import tpu as pltpu

HEAD_DIM = 128
CROSS_HEADS = 4
CROSS_WIDTH = CROSS_HEADS * HEAD_DIM
CONV_WIDTH = 3
DIFF_HEAD_DIM = 64
ROPE_THETA = 10000.0
RMS_EPS = 1e-6
NEG_BIG = -1e30
N_MIXERS = 2
LOG2_E = math.log2(math.e)

LANES = 128
BF16_SUBLANES = 16
CONV_PAD = 8
MXU_DEPTH = 256
ROW_PARTS = 2
SCORE_BUFFERS = 2
VMEM_LIMIT_BYTES = 56 * 1024 * 1024

IN_PROJ_ROWS = 512
IN_PROJ_VMEM_LIMIT_BYTES = 60 * 1024 * 1024
IN_PROJ_COLS = 1024
ROPE_TABLE_ROWS = 1024
ATTN_TILE = 512
MIX_ROWS = 512

F32 = jnp.float32
BF16 = jnp.bfloat16


def _rms_scale(xf):
    return xf * lax.rsqrt(jnp.mean(xf * xf, axis=-1, keepdims=True) + RMS_EPS)


def _rope_table_kernel(pos_ref, inv_ref, sign_ref, cos_ref, sin_ref):
    rows = pos_ref.shape[0]
    half = DIFF_HEAD_DIM // 2
    groups = LANES // half
    ang = pos_ref[...].astype(F32) * inv_ref[...]
    group_of_lane = lax.broadcasted_iota(jnp.int32, ang.shape, 1) // half
    for table, sign, out_ref in ((jnp.cos(ang), None, cos_ref),
                                 (jnp.sin(ang), sign_ref[...], sin_ref)):
        shifted = [table] + [pltpu.roll(table, d * half, 1) for d in range(1, groups)]
        for k in range(groups):
            g = shifted[(groups - k) % groups]
            for j in range(1, groups):
                g = jnp.where(group_of_lane == j, shifted[(j - k) % groups], g)
            out_ref[k * rows:(k + 1) * rows, :] = g if sign is None else g * sign


def _rope_tables(positions, tm):
    m = positions.size
    half = DIFF_HEAD_DIM // 2
    groups = LANES // half
    rows = tm // groups
    inv_freq = ROPE_THETA ** (-jnp.arange(0, DIFF_HEAD_DIM, 2, dtype=F32) / DIFF_HEAD_DIM)
    inv = jnp.tile(inv_freq, groups).reshape(1, LANES)
    sign = jnp.tile(jnp.concatenate([-jnp.ones((half,), F32), jnp.ones((half,), F32)]),
                    LANES // DIFF_HEAD_DIM).reshape(1, LANES)
    pos = positions.reshape(m // tm, groups, rows).transpose(0, 2, 1)
    pos = jnp.repeat(pos, half, axis=2).reshape(m // groups, LANES)
    return pl.pallas_call(
        _rope_table_kernel,
        grid=(m // tm,),
        in_specs=[pl.BlockSpec((rows, LANES), lambda i: (i, 0)),
                  pl.BlockSpec((1, LANES), lambda i: (0, 0)),
                  pl.BlockSpec((1, LANES), lambda i: (0, 0))],
        out_specs=[pl.BlockSpec((tm, LANES), lambda i: (i, 0)),
                   pl.BlockSpec((tm, LANES), lambda i: (i, 0))],
        out_shape=[jax.ShapeDtypeStruct((m, LANES), F32),
                   jax.ShapeDtypeStruct((m, LANES), F32)],
        name="rope_tables",
    )(pos, inv, sign)


def _in_proj_kernel(*refs, rope_tiles, tn):
    if rope_tiles:
        x_ref, g_ref, w_ref, cos_ref, sin_ref, o_ref, h_ref = refs
    else:
        x_ref, g_ref, w_ref, o_ref, h_ref = refs

    def emit(c, acc, rotary):
        base = c * tn
        if not rotary:
            o_ref[:, base:base + tn] = acc.astype(o_ref.dtype)
            return
        cos = cos_ref[...]
        sin = sin_ref[...]
        lane = lax.broadcasted_iota(jnp.int32, cos.shape, 1)
        first_half = (lane % DIFF_HEAD_DIM) < (DIFF_HEAD_DIM // 2)
        half = DIFF_HEAD_DIM // 2
        for k in range(tn // LANES):
            t = acc[:, k * LANES:(k + 1) * LANES]
            partner = jnp.where(first_half, pltpu.roll(t, LANES - half, 1), pltpu.roll(t, half, 1))
            o_ref[:, base + k * LANES:base + (k + 1) * LANES] = (
                t * cos + partner * sin).astype(o_ref.dtype)

    h_ref[...] = (_rms_scale(x_ref[...]) * g_ref[...]).astype(BF16)
    for c in range(w_ref.shape[1] // tn):
        acc = jnp.dot(h_ref[...], w_ref[:, c * tn:(c + 1) * tn], preferred_element_type=F32)
        emit(c, acc, c < rope_tiles)


def _in_proj(x2d, g_all, w_all_bf16, layer, rope, *, tm, tn, rope_cols):
    m, d = x2d.shape
    n = w_all_bf16.shape[2]
    rope_tiles = 0 if rope is None else rope_cols // tn
    in_specs = [pl.BlockSpec((tm, d), lambda i: (i, 0)),
                pl.BlockSpec((None, 1, d), lambda i: (layer, 0, 0)),
                pl.BlockSpec((None, d, n), lambda i: (layer, 0, 0), pipeline_mode=pl.Buffered(1))]
    args = [x2d, g_all, w_all_bf16]
    if rope is not None:
        assert rope_cols % tn == 0
        in_specs += [pl.BlockSpec((tm, LANES), lambda i: (i, 0)),
                     pl.BlockSpec((tm, LANES), lambda i: (i, 0))]
        args += list(rope)
    return pl.pallas_call(
        functools.partial(_in_proj_kernel, rope_tiles=rope_tiles, tn=tn),
        grid=(m // tm,),
        in_specs=in_specs,
        out_specs=pl.BlockSpec((tm, n), lambda i: (i, 0)),
        out_shape=jax.ShapeDtypeStruct((m, n), BF16),
        scratch_shapes=[pltpu.VMEM((tm, d), BF16)],
        compiler_params=pltpu.CompilerParams(
            dimension_semantics=("arbitrary",),
            vmem_limit_bytes=IN_PROJ_VMEM_LIMIT_BYTES),
        name="in_proj_rope" if rope_tiles else "in_proj",
    )(*args)


def _mem_kv_kernel(mem_ref, g_ref, w_ref, o_ref):
    h = (_rms_scale(mem_ref[...]) * g_ref[...]).astype(BF16)
    o_ref[...] = jnp.dot(h, w_ref[...], preferred_element_type=F32).astype(o_ref.dtype)


def _mem_kv(mem2d, g_all, w_all_bf16, layer, *, mem_len):
    m, d = mem2d.shape
    n = w_all_bf16.shape[2]
    return pl.pallas_call(
        _mem_kv_kernel,
        grid=(m // mem_len,),
        in_specs=[pl.BlockSpec((mem_len, d), lambda i: (i, 0)),
                  pl.BlockSpec((None, 1, d), lambda i: (layer, 0, 0)),
                  pl.BlockSpec((None, d, n), lambda i: (layer, 0, 0))],
        out_specs=pl.BlockSpec((mem_len, n), lambda i: (i, 0)),
        out_shape=jax.ShapeDtypeStruct((m, n), BF16),
        compiler_params=pltpu.CompilerParams(
            dimension_semantics=("arbitrary",), vmem_limit_bytes=VMEM_LIMIT_BYTES),
        name="mem_kv",
    )(mem2d, g_all, w_all_bf16)


def _diff_attn_kernel(zero_ref, lam_ref, q_ref, k_ref, v_ref, g_ref, o_ref, vt_ref, *scratch,
                      t, lambda_init):
    s_refs = scratch[:SCORE_BUFFERS]
    mx_refs = scratch[SCORE_BUFFERS:2 * SCORE_BUFFERS]
    p_ref, acc1_ref, acc2_ref = scratch[2 * SCORE_BUFFERS:]
    nq = q_ref.shape[0] // t
    n_maps = 2
    acc_refs = (acc1_ref, acc2_ref)
    bufs = tuple(zip(s_refs, mx_refs))
    z = zero_ref[0]
    half = t // 2

    ones = jnp.ones((BF16_SUBLANES, t), BF16)
    for j in range(nq):
        vt_ref[0:HEAD_DIM, j * t:(j + 1) * t] = (
            v_ref[j * t:(j + 1) * t, :].astype(F32).T.astype(BF16))
        vt_ref[HEAD_DIM:, j * t:(j + 1) * t] = ones

    lp = lam_ref[...]
    lam = (jnp.exp(jnp.sum(lp[0:1] * lp[1:2], axis=1, keepdims=True))
           - jnp.exp(jnp.sum(lp[2:3] * lp[3:4], axis=1, keepdims=True)) + lambda_init)

    def visible(n_keys, n_queries):
        key = lax.broadcasted_iota(jnp.int32, (n_keys, n_queries), 0)
        qry = lax.broadcasted_iota(jnp.int32, (n_keys, n_queries), 1)
        return key <= qry

    def q_maps_of(qi):
        q_t = (q_ref[qi * t:(qi + 1) * t, :].astype(F32) * (LOG2_E * DIFF_HEAD_DIM ** -0.5)).T
        row = lax.broadcasted_iota(jnp.int32, q_t.shape, 0)
        zero = jnp.zeros_like(q_t)
        return (jnp.where(row < DIFF_HEAD_DIM, q_t, zero).astype(BF16),
                jnp.where(row >= DIFF_HEAD_DIM, q_t, zero).astype(BF16))

    def scores(q_maps, j, diagonal, buf):
        s_ref, mx_ref = buf
        for mi in range(n_maps):
            if not diagonal:
                s = jnp.dot(k_ref[j * t:(j + 1) * t, :], q_maps[mi],
                            preferred_element_type=F32)
                s_ref[z + mi] = s
                mx_ref[mi] = jnp.max(s, axis=0, keepdims=True)
                continue
            s_top = jnp.dot(k_ref[j * t:j * t + half, :], q_maps[mi],
                            preferred_element_type=F32)
            s_top = jnp.where(visible(half, t), s_top, NEG_BIG)
            s_bot = jnp.dot(k_ref[j * t + half:(j + 1) * t, :], q_maps[mi][:, half:],
                            preferred_element_type=F32)
            s_bot = jnp.where(visible(half, half), s_bot, NEG_BIG)
            s_ref[z + mi, 0:half, :] = s_top
            s_ref[z + mi, half:, half:] = s_bot
            mx_top = jnp.max(s_top, axis=0, keepdims=True)
            mx_ref[mi, :, 0:half] = mx_top[:, :half]
            mx_ref[mi, :, half:] = jnp.maximum(mx_top[:, half:],
                                               jnp.max(s_bot, axis=0, keepdims=True))

    def consume(j, diagonal, buf, m_old):
        s_ref, mx_ref = buf
        v_t = vt_ref[:, j * t:(j + 1) * t]
        m_out = []
        for mi in range(n_maps):
            acc_ref = acc_refs[mi]
            if m_old is None:
                m_new = mx_ref[mi]
                alpha = None
            else:
                m_new = jnp.maximum(m_old[mi], mx_ref[mi])
                alpha = jnp.exp2(m_old[mi] - m_new)
            if not diagonal:
                p_ref[z + mi] = jnp.exp2(s_ref[z + mi] - m_new).astype(BF16)
                pv = jnp.dot(v_t, p_ref[z + mi], preferred_element_type=F32)
                acc_ref[...] = pv if alpha is None else alpha * acc_ref[...] + pv
            else:
                p_ref[z + mi, 0:half, :] = jnp.exp2(
                    s_ref[z + mi, 0:half, :] - m_new).astype(BF16)
                p_ref[z + mi, half:, half:] = jnp.exp2(
                    s_ref[z + mi, half:, half:] - m_new[:, half:]).astype(BF16)
                pv = jnp.dot(v_t[:, :half], p_ref[z + mi, 0:half, :],
                             preferred_element_type=F32)
                pv_late = jnp.dot(v_t[:, half:], p_ref[z + mi, half:, half:],
                                  preferred_element_type=F32)
                if alpha is None:
                    acc_ref[:, :half] = pv[:, :half]
                    acc_ref[:, half:] = pv[:, half:] + pv_late
                else:
                    acc_ref[:, :half] = alpha[:, :half] * acc_ref[:, :half] + pv[:, :half]
                    acc_ref[:, half:] = (alpha[:, half:] * acc_ref[:, half:]
                                         + pv[:, half:] + pv_late)
            m_out.append(m_new)
        return m_out

    def finalize(qi):
        o1 = acc1_ref[0:HEAD_DIM, :] / acc1_ref[HEAD_DIM:HEAD_DIM + 1, :]
        o2 = acc2_ref[0:HEAD_DIM, :] / acc2_ref[HEAD_DIM:HEAD_DIM + 1, :]
        o = o1 - lam * o2
        o = o * lax.rsqrt(jnp.mean(o * o, axis=0, keepdims=True) + RMS_EPS)
        o_ref[qi * t:(qi + 1) * t, :] = (
            (o.T * g_ref[...]) * (1.0 - lambda_init)).astype(o_ref.dtype)

    tiles = [(qi, j) for qi in range(nq) for j in range(qi + 1)]
    ahead = SCORE_BUFFERS - 1
    q_maps_row = {}

    def issue_scores(i):
        qn, jn = tiles[i]
        if qn not in q_maps_row:
            q_maps_row[qn] = q_maps_of(qn)
        scores(q_maps_row[qn], jn, jn == qn, bufs[i % SCORE_BUFFERS])

    for i in range(min(ahead, len(tiles))):
        issue_scores(i)
    m_run = None
    for i, (qi, j) in enumerate(tiles):
        if i + ahead < len(tiles):
            issue_scores(i + ahead)
        m_run = consume(j, j == qi, bufs[i % SCORE_BUFFERS], m_run)
        if j == qi:
            finalize(qi)
            m_run = None


def _diff_attention(proj, lam_params, subln_g, *, batch, seq, heads, layer_idx, t):
    m = proj.shape[0]
    k_off = heads
    v_off = 2 * heads
    lambda_init = 0.8 - 0.6 * math.exp(-0.3 * layer_idx)
    acc_rows = HEAD_DIM + BF16_SUBLANES
    return pl.pallas_call(
        functools.partial(_diff_attn_kernel, t=t, lambda_init=lambda_init),
        grid=(batch, heads),
        in_specs=[pl.BlockSpec(memory_space=pltpu.SMEM),
                  pl.BlockSpec(lam_params.shape, lambda b, h: (0, 0)),
                  pl.BlockSpec((seq, HEAD_DIM), lambda b, h: (b, h)),
                  pl.BlockSpec((seq, HEAD_DIM), lambda b, h: (b, k_off + h)),
                  pl.BlockSpec((seq, HEAD_DIM), lambda b, h: (b, v_off + h)),
                  pl.BlockSpec((1, HEAD_DIM), lambda b, h: (0, 0))],
        out_specs=pl.BlockSpec((seq, HEAD_DIM), lambda b, h: (b, h)),
        out_shape=jax.ShapeDtypeStruct((m, heads * HEAD_DIM), BF16),
        scratch_shapes=([pltpu.VMEM((acc_rows, seq), BF16)]
                        + [pltpu.VMEM((2, t, t), F32)] * SCORE_BUFFERS
                        + [pltpu.VMEM((2, 1, t), F32)] * SCORE_BUFFERS
                        + [pltpu.VMEM((2, t, t), BF16),
                           pltpu.VMEM((acc_rows, t), F32), pltpu.VMEM((acc_rows, t), F32)]),
        compiler_params=pltpu.CompilerParams(
            dimension_semantics=("arbitrary", "arbitrary"),
            vmem_limit_bytes=VMEM_LIMIT_BYTES),
        name="diff_attention",
    )(jnp.zeros((1,), jnp.int32), lam_params, proj, proj, proj, subln_g.reshape(1, HEAD_DIM))


def _silu(g):
    return g * (1.0 / (1.0 + jnp.exp(-g)))


def _mix_out_kernel(*refs, conv, tm, tok_width, tiles_per_seq, mem_len):
    if conv:
        (xin_ref, gb_ref, gc_ref, cw_ref, qm_ref, gate_a_ref, gate_b_ref, kv_ref, wout_ref,
         x_ref, gpost_ref, o_ref, br_ref, y_ref, u_ref) = refs
    else:
        (tok_ref, qm_ref, gate_a_ref, gate_b_ref, kv_ref, wout_ref,
         x_ref, gpost_ref, o_ref, br_ref, y_ref) = refs
    half_gate = gate_a_ref.shape[1]
    chunk = MXU_DEPTH
    pad = CONV_PAD

    def gate_cols(rows, lo, hi):
        if hi <= half_gate:
            return _silu(gate_a_ref[rows, lo:hi])
        assert lo >= half_gate
        return _silu(gate_b_ref[rows, lo - half_gate:hi - half_gate])

    if conv:
        first = (pl.program_id(0) % tiles_per_seq) == 0

        @pl.when(first)
        def _():
            u_ref[0:pad, :] = jnp.zeros((pad, tok_width), F32)

        @pl.when(jnp.logical_not(first))
        def _():
            u_ref[0:pad, :] = u_ref[tm:tm + pad, :]

        u_ref[pad:pad + tm, :] = gc_ref[...].astype(F32) * xin_ref[...].astype(F32)

    def rows_part(r0, r1):
        rows = slice(r0, r1)

        def project(lo, hi, first_chunk):
            part = jnp.dot(br_ref[rows, lo:hi], wout_ref[lo:hi, :], preferred_element_type=F32)
            if first_chunk:
                y_ref[rows, :] = part
            else:
                y_ref[rows, :] += part

        for lo in range(0, tok_width, chunk):
            hi = lo + chunk
            if conv:
                w = cw_ref[:, lo:hi]
                conv_out = (u_ref[pad + r0:pad + r1, lo:hi] * w[2:3]
                            + u_ref[pad - 1 + r0:pad - 1 + r1, lo:hi] * w[1:2]
                            + u_ref[pad - 2 + r0:pad - 2 + r1, lo:hi] * w[0:1])
                tok = (gb_ref[rows, lo:hi].astype(F32) * conv_out).astype(BF16)
            else:
                tok = tok_ref[rows, lo:hi]
            br_ref[rows, lo:hi] = tok * gate_cols(rows, lo, hi)
            project(lo, hi, lo == 0)

        scale = HEAD_DIM ** -0.5
        for h in range(CROSS_HEADS):
            lo = h * HEAD_DIM
            q_h = qm_ref[rows, lo:lo + HEAD_DIM]
            k_h = kv_ref[:, lo:lo + HEAD_DIM]
            v_h = kv_ref[:, CROSS_WIDTH + lo:CROSS_WIDTH + lo + HEAD_DIM]
            s = lax.dot_general(q_h, k_h, (((1,), (1,)), ((), ())),
                                preferred_element_type=F32) * scale
            p = jnp.exp(s - jnp.max(s, axis=-1, keepdims=True))
            o_h = jnp.dot(p.astype(BF16), v_h, preferred_element_type=F32)
            o_h = o_h / jnp.sum(p, axis=-1, keepdims=True)
            col = tok_width + lo
            br_ref[rows, col:col + HEAD_DIM] = (
                o_h.astype(BF16) * gate_cols(rows, col, col + HEAD_DIM))
        for lo in range(tok_width, tok_width + CROSS_WIDTH, chunk):
            project(lo, lo + chunk, False)

        o_ref[rows, :] = x_ref[rows, :] + _rms_scale(y_ref[rows, :]) * gpost_ref[...]

    part_rows = tm // ROW_PARTS
    for r0 in range(0, tm, part_rows):
        rows_part(r0, r0 + part_rows)


def _mix_out(proj, tok, conv_w, kv, w_out_all_bf16, layer, x2d, g_post_all, *, seq, mem_len, tm):
    m, d = x2d.shape
    mix_width = w_out_all_bf16.shape[1]
    tok_width = mix_width - CROSS_WIDTH
    conv = tok is None
    tiles_per_seq = seq // tm
    qm_blk = (3 * tok_width) // CROSS_WIDTH
    half_gate = mix_width // 2
    gate_blk = (3 * tok_width + CROSS_WIDTH) // half_gate
    assert qm_blk * CROSS_WIDTH == 3 * tok_width
    assert gate_blk * half_gate == 3 * tok_width + CROSS_WIDTH

    tail_specs = [pl.BlockSpec((tm, CROSS_WIDTH), lambda i: (i, qm_blk)),
                  pl.BlockSpec((tm, half_gate), lambda i: (i, gate_blk)),
                  pl.BlockSpec((tm, half_gate), lambda i: (i, gate_blk + 1)),
                  pl.BlockSpec((mem_len, 2 * CROSS_WIDTH), lambda i: (i // tiles_per_seq, 0)),
                  pl.BlockSpec((None, mix_width, d), lambda i: (layer, 0, 0),
                               pipeline_mode=pl.Buffered(1)),
                  pl.BlockSpec((tm, d), lambda i: (i, 0)),
                  pl.BlockSpec((None, 1, d), lambda i: (layer, 0, 0))]
    tail_args = [proj, proj, proj, kv, w_out_all_bf16, x2d, g_post_all]
    scratch = [pltpu.VMEM((tm, mix_width), BF16), pltpu.VMEM((tm, d), F32)]
    if conv:
        head_specs = [pl.BlockSpec((tm, tok_width), lambda i: (i, 0)),
                      pl.BlockSpec((tm, tok_width), lambda i: (i, 1)),
                      pl.BlockSpec((tm, tok_width), lambda i: (i, 2)),
                      pl.BlockSpec((CONV_WIDTH, tok_width), lambda i: (0, 0))]
        head_args = [proj, proj, proj, conv_w.T]
        scratch.append(pltpu.VMEM((tm + CONV_PAD, tok_width), F32))
    else:
        head_specs = [pl.BlockSpec((tm, tok_width), lambda i: (i, 0))]
        head_args = [tok]
    return pl.pallas_call(
        functools.partial(_mix_out_kernel, conv=conv, tm=tm, tok_width=tok_width,
                          tiles_per_seq=tiles_per_seq, mem_len=mem_len),
        grid=(m // tm,),
        in_specs=head_specs + tail_specs,
        out_specs=pl.BlockSpec((tm, d), lambda i: (i, 0)),
        out_shape=jax.ShapeDtypeStruct((m, d), F32),
        scratch_shapes=scratch,
        compiler_params=pltpu.CompilerParams(
            dimension_semantics=("arbitrary",), vmem_limit_bytes=VMEM_LIMIT_BYTES),
        name="mix_out_conv" if conv else "mix_out_attn",
    )(*head_args, *tail_args)


def kernel(x, mem, positions, pre_norm, post_norm, mem_norm, w_in, w_kv_mem, w_out,
           conv_w, diff_lambda, diff_subln):
    batch, seq, d = x.shape
    mem_len = mem.shape[1]
    depth = w_in.shape[0]
    tok_width = w_out.shape[1] - CROSS_WIDTH
    heads = tok_width // HEAD_DIM
    x2d = x.reshape(batch * seq, d)
    mem2d = mem.reshape(batch * mem_len, d)
    rope = _rope_tables(positions, tm=ROPE_TABLE_ROWS) if depth > 1 else None
    w_in_bf16 = w_in.astype(BF16)
    w_kv_bf16 = w_kv_mem.astype(BF16)
    w_out_bf16 = w_out.astype(BF16)
    pre_g = pre_norm.reshape(depth, 1, d)
    post_g = post_norm.reshape(depth, 1, d)
    mem_g = mem_norm.reshape(depth, 1, d)

    for i in range(depth):
        attn_layer = (i % N_MIXERS) == 1
        proj = _in_proj(x2d, pre_g, w_in_bf16, i, rope if attn_layer else None,
                        tm=IN_PROJ_ROWS, tn=IN_PROJ_COLS, rope_cols=2 * tok_width)
        kv = _mem_kv(mem2d, mem_g, w_kv_bf16, i, mem_len=mem_len)
        if attn_layer:
            tok = _diff_attention(proj, diff_lambda[i // N_MIXERS], diff_subln[i // N_MIXERS],
                                  batch=batch, seq=seq, heads=heads, layer_idx=i, t=ATTN_TILE)
            conv = None
        else:
            tok = None
            conv = conv_w[i // N_MIXERS]
        x2d = _mix_out(proj, tok, conv, kv, w_out_bf16, i, x2d, post_g,
                       seq=seq, mem_len=mem_len, tm=MIX_ROWS)
    return x2d.reshape(batch, seq, d)
```

```python
import functools
import math

import jax
import jax.numpy as jnp
from jax import lax
from jax.experimental import pallas as pl
from jax.experimental.pallas import tpu as pltpu

HEAD_DIM = 128
CROSS_HEADS = 4
CROSS_WIDTH = CROSS_HEADS * HEAD_DIM
CONV_WIDTH = 3
DIFF_HEAD_DIM = 64
ROPE_THETA = 10000.0
RMS_EPS = 1e-6
NEG_BIG = -1e30
N_MIXERS = 2
LOG2_E = math.log2(math.e)

LANES = 128
BF16_SUBLANES = 16
CONV_PAD = 8
MXU_DEPTH = 256
ROW_PARTS = 2
SCORE_BUFFERS = 2
VMEM_LIMIT_BYTES = 56 * 1024 * 1024

IN_PROJ_ROWS = 512
IN_PROJ_VMEM_LIMIT_BYTES = 60 * 1024 * 1024
W_STAGE_COLS = 256
IN_PROJ_COLS = 1024
ROPE_TABLE_ROWS = 1024
ATTN_TILE = 512
MIX_ROWS = 512

F32 = jnp.float32
BF16 = jnp.bfloat16


def _rms_scale(xf):
    return xf * lax.rsqrt(jnp.mean(xf * xf, axis=-1, keepdims=True) + RMS_EPS)


def _rope_table_kernel(pos_ref, inv_ref, sign_ref, cos_ref, sin_ref):
    rows = pos_ref.shape[0]
    half = DIFF_HEAD_DIM // 2
    groups = LANES // half
    ang = pos_ref[...].astype(F32) * inv_ref[...]
    group_of_lane = lax.broadcasted_iota(jnp.int32, ang.shape, 1) // half
    for table, sign, out_ref in ((jnp.cos(ang), None, cos_ref),
                                 (jnp.sin(ang), sign_ref[...], sin_ref)):
        shifted = [table] + [pltpu.roll(table, d * half, 1) for d in range(1, groups)]
        for k in range(groups):
            g = shifted[(groups - k) % groups]
            for j in range(1, groups):
                g = jnp.where(group_of_lane == j, shifted[(j - k) % groups], g)
            out_ref[k * rows:(k + 1) * rows, :] = g if sign is None else g * sign


def _rope_tables(positions, tm):
    m = positions.size
    half = DIFF_HEAD_DIM // 2
    groups = LANES // half
    rows = tm // groups
    inv_freq = ROPE_THETA ** (-jnp.arange(0, DIFF_HEAD_DIM, 2, dtype=F32) / DIFF_HEAD_DIM)
    inv = jnp.tile(inv_freq, groups).reshape(1, LANES)
    sign = jnp.tile(jnp.concatenate([-jnp.ones((half,), F32), jnp.ones((half,), F32)]),
                    LANES // DIFF_HEAD_DIM).reshape(1, LANES)
    pos = positions.reshape(m // tm, groups, rows).transpose(0, 2, 1)
    pos = jnp.repeat(pos, half, axis=2).reshape(m // groups, LANES)
    return pl.pallas_call(
        _rope_table_kernel,
        grid=(m // tm,),
        in_specs=[pl.BlockSpec((rows, LANES), lambda i: (i, 0)),
                  pl.BlockSpec((1, LANES), lambda i: (0, 0)),
                  pl.BlockSpec((1, LANES), lambda i: (0, 0))],
        out_specs=[pl.BlockSpec((tm, LANES), lambda i: (i, 0)),
                   pl.BlockSpec((tm, LANES), lambda i: (i, 0))],
        out_shape=[jax.ShapeDtypeStruct((m, LANES), F32),
                   jax.ShapeDtypeStruct((m, LANES), F32)],
        name="rope_tables",
    )(pos, inv, sign)


def _in_proj_kernel(*refs, rope_tiles, tn, layer):
    if rope_tiles:
        x_ref, g_ref, w_hbm, cos_ref, sin_ref, o_ref, h_ref, w_ref, stage_ref, sem = refs
    else:
        x_ref, g_ref, w_hbm, o_ref, h_ref, w_ref, stage_ref, sem = refs

    def chunk_copy(c, slot):
        return pltpu.make_async_copy(
            w_hbm.at[layer, :, pl.ds(c * W_STAGE_COLS, W_STAGE_COLS)],
            stage_ref.at[slot], sem.at[slot])

    @pl.when(pl.program_id(0) == 0)
    def _():
        n_chunks = w_ref.shape[1] // W_STAGE_COLS
        chunk_copy(0, 0).start()
        for c in range(n_chunks):
            slot = c % 2
            if c + 1 < n_chunks:
                chunk_copy(c + 1, 1 - slot).start()
            chunk_copy(c, slot).wait()
            w_ref[:, c * W_STAGE_COLS:(c + 1) * W_STAGE_COLS] = stage_ref[slot].astype(BF16)

    def emit(c, acc, rotary):
        base = c * tn
        if not rotary:
            o_ref[:, base:base + tn] = acc.astype(o_ref.dtype)
            return
        cos = cos_ref[...]
        sin = sin_ref[...]
        lane = lax.broadcasted_iota(jnp.int32, cos.shape, 1)
        first_half = (lane % DIFF_HEAD_DIM) < (DIFF_HEAD_DIM // 2)
        half = DIFF_HEAD_DIM // 2
        for k in range(tn // LANES):
            t = acc[:, k * LANES:(k + 1) * LANES]
            partner = jnp.where(first_half, pltpu.roll(t, LANES - half, 1), pltpu.roll(t, half, 1))
            o_ref[:, base + k * LANES:base + (k + 1) * LANES] = (
                t * cos + partner * sin).astype(o_ref.dtype)

    h_ref[...] = (_rms_scale(x_ref[...]) * g_ref[...]).astype(BF16)
    for c in range(w_ref.shape[1] // tn):
        acc = jnp.dot(h_ref[...], w_ref[:, c * tn:(c + 1) * tn], preferred_element_type=F32)
        emit(c, acc, c < rope_tiles)


def _in_proj(x2d, g_all, w_all, layer, rope, *, tm, tn, rope_cols):
    m, d = x2d.shape
    n = w_all.shape[2]
    assert n % W_STAGE_COLS == 0
    rope_tiles = 0 if rope is None else rope_cols // tn
    in_specs = [pl.BlockSpec((tm, d), lambda i: (i, 0)),
                pl.BlockSpec((None, 1, d), lambda i: (layer, 0, 0)),
                pl.BlockSpec(memory_space=pl.ANY)]
    args = [x2d, g_all, w_all]
    if rope is not None:
        assert rope_cols % tn == 0
        in_specs += [pl.BlockSpec((tm, LANES), lambda i: (i, 0)),
                     pl.BlockSpec((tm, LANES), lambda i: (i, 0))]
        args += list(rope)
    return pl.pallas_call(
        functools.partial(_in_proj_kernel, rope_tiles=rope_tiles, tn=tn, layer=layer),
        grid=(m // tm,),
        in_specs=in_specs,
        out_specs=pl.BlockSpec((tm, n), lambda i: (i, 0)),
        out_shape=jax.ShapeDtypeStruct((m, n), BF16),
        scratch_shapes=[pltpu.VMEM((tm, d), BF16), pltpu.VMEM((d, n), BF16),
                        pltpu.VMEM((2, d, W_STAGE_COLS), F32), pltpu.SemaphoreType.DMA((2,))],
        compiler_params=pltpu.CompilerParams(
            dimension_semantics=("arbitrary",),
            vmem_limit_bytes=IN_PROJ_VMEM_LIMIT_BYTES),
        name="in_proj_rope" if rope_tiles else "in_proj",
    )(*args)


def _mem_kv_kernel(mem_ref, g_ref, w_ref, o_ref):
    h = (_rms_scale(mem_ref[...]) * g_ref[...]).astype(BF16)
    o_ref[...] = jnp.dot(h, w_ref[...], preferred_element_type=F32).astype(o_ref.dtype)


def _mem_kv(mem2d, g_all, w_all_bf16, layer, *, mem_len):
    m, d = mem2d.shape
    n = w_all_bf16.shape[2]
    return pl.pallas_call(
        _mem_kv_kernel,
        grid=(m // mem_len,),
        in_specs=[pl.BlockSpec((mem_len, d), lambda i: (i, 0)),
                  pl.BlockSpec((None, 1, d), lambda i: (layer, 0, 0)),
                  pl.BlockSpec((None, d, n), lambda i: (layer, 0, 0))],
        out_specs=pl.BlockSpec((mem_len, n), lambda i: (i, 0)),
        out_shape=jax.ShapeDtypeStruct((m, n), BF16),
        compiler_params=pltpu.CompilerParams(
            dimension_semantics=("arbitrary",), vmem_limit_bytes=VMEM_LIMIT_BYTES),
        name="mem_kv",
    )(mem2d, g_all, w_all_bf16)


def _diff_attn_kernel(zero_ref, lam_ref, q_ref, k_ref, v_ref, g_ref, o_ref, vt_ref, *scratch,
                      t, lambda_init):
    s_refs = scratch[:SCORE_BUFFERS]
    mx_refs = scratch[SCORE_BUFFERS:2 * SCORE_BUFFERS]
    p_ref, acc1_ref, acc2_ref = scratch[2 * SCORE_BUFFERS:]
    nq = q_ref.shape[0] // t
    n_maps = 2
    acc_refs = (acc1_ref, acc2_ref)
    bufs = tuple(zip(s_refs, mx_refs))
    z = zero_ref[0]
    half = t // 2

    ones = jnp.ones((BF16_SUBLANES, t), BF16)
    for j in range(nq):
        vt_ref[0:HEAD_DIM, j * t:(j + 1) * t] = (
            v_ref[j * t:(j + 1) * t, :].astype(F32).T.astype(BF16))
        vt_ref[HEAD_DIM:, j * t:(j + 1) * t] = ones

    lp = lam_ref[...]
    lam = (jnp.exp(jnp.sum(lp[0:1] * lp[1:2], axis=1, keepdims=True))
           - jnp.exp(jnp.sum(lp[2:3] * lp[3:4], axis=1, keepdims=True)) + lambda_init)

    def visible(n_keys, n_queries):
        key = lax.broadcasted_iota(jnp.int32, (n_keys, n_queries), 0)
        qry = lax.broadcasted_iota(jnp.int32, (n_keys, n_queries), 1)
        return key <= qry

    def q_maps_of(qi):
        q_t = (q_ref[qi * t:(qi + 1) * t, :].astype(F32) * (LOG2_E * DIFF_HEAD_DIM ** -0.5)).T
        row = lax.broadcasted_iota(jnp.int32, q_t.shape, 0)
        zero = jnp.zeros_like(q_t)
        return (jnp.where(row < DIFF_HEAD_DIM, q_t, zero).astype(BF16),
                jnp.where(row >= DIFF_HEAD_DIM, q_t, zero).astype(BF16))

    def scores(q_maps, j, diagonal, buf):
        s_ref, mx_ref = buf
        for mi in range(n_maps):
            if not diagonal:
                s = jnp.dot(k_ref[j * t:(j + 1) * t, :], q_maps[mi],
                            preferred_element_type=F32)
                s_ref[z + mi] = s
                mx_ref[mi] = jnp.max(s, axis=0, keepdims=True)
                continue
            s_top = jnp.dot(k_ref[j * t:j * t + half, :], q_maps[mi],
                            preferred_element_type=F32)
            s_top = jnp.where(visible(half, t), s_top, NEG_BIG)
            s_bot = jnp.dot(k_ref[j * t + half:(j + 1) * t, :], q_maps[mi][:, half:],
                            preferred_element_type=F32)
            s_bot = jnp.where(visible(half, half), s_bot, NEG_BIG)
            s_ref[z + mi, 0:half, :] = s_top
            s_ref[z + mi, half:, half:] = s_bot
            mx_top = jnp.max(s_top, axis=0, keepdims=True)
            mx_ref[mi, :, 0:half] = mx_top[:, :half]
            mx_ref[mi, :, half:] = jnp.maximum(mx_top[:, half:],
                                               jnp.max(s_bot, axis=0, keepdims=True))

    def consume(j, diagonal, buf, m_old):
        s_ref, mx_ref = buf
        v_t = vt_ref[:, j * t:(j + 1) * t]
        m_out = []
        for mi in range(n_maps):
            acc_ref = acc_refs[mi]
            if m_old is None:
                m_new = mx_ref[mi]
                alpha = None
            else:
                m_new = jnp.maximum(m_old[mi], mx_ref[mi])
                alpha = jnp.exp2(m_old[mi] - m_new)
            if not diagonal:
                p_ref[z + mi] = jnp.exp2(s_ref[z + mi] - m_new).astype(BF16)
                pv = jnp.dot(v_t, p_ref[z + mi], preferred_element_type=F32)
                acc_ref[...] = pv if alpha is None else alpha * acc_ref[...] + pv
            else:
                p_ref[z + mi, 0:half, :] = jnp.exp2(
                    s_ref[z + mi, 0:half, :] - m_new).astype(BF16)
                p_ref[z + mi, half:, half:] = jnp.exp2(
                    s_ref[z + mi, half:, half:] - m_new[:, half:]).astype(BF16)
                pv = jnp.dot(v_t[:, :half], p_ref[z + mi, 0:half, :],
                             preferred_element_type=F32)
                pv_late = jnp.dot(v_t[:, half:], p_ref[z + mi, half:, half:],
                                  preferred_element_type=F32)
                if alpha is None:
                    acc_ref[:, :half] = pv[:, :half]
                    acc_ref[:, half:] = pv[:, half:] + pv_late
                else:
                    acc_ref[:, :half] = alpha[:, :half] * acc_ref[:, :half] + pv[:, :half]
                    acc_ref[:, half:] = (alpha[:, half:] * acc_ref[:, half:]
                                         + pv[:, half:] + pv_late)
            m_out.append(m_new)
        return m_out

    def finalize(qi):
        o1 = acc1_ref[0:HEAD_DIM, :] / acc1_ref[HEAD_DIM:HEAD_DIM + 1, :]
        o2 = acc2_ref[0:HEAD_DIM, :] / acc2_ref[HEAD_DIM:HEAD_DIM + 1, :]
        o = o1 - lam * o2
        o = o * lax.rsqrt(jnp.mean(o * o, axis=0, keepdims=True) + RMS_EPS)
        o_ref[qi * t:(qi + 1) * t, :] = (
            (o.T * g_ref[...]) * (1.0 - lambda_init)).astype(o_ref.dtype)

    tiles = [(qi, j) for qi in range(nq) for j in range(qi + 1)]
    ahead = SCORE_BUFFERS - 1
    q_maps_row = {}

    def issue_scores(i):
        qn, jn = tiles[i]
        if qn not in q_maps_row:
            q_maps_row[qn] = q_maps_of(qn)
        scores(q_maps_row[qn], jn, jn == qn, bufs[i % SCORE_BUFFERS])

    for i in range(min(ahead, len(tiles))):
        issue_scores(i)
    m_run = None
    for i, (qi, j) in enumerate(tiles):
        if i + ahead < len(tiles):
            issue_scores(i + ahead)
        m_run = consume(j, j == qi, bufs[i % SCORE_BUFFERS], m_run)
        if j == qi:
            finalize(qi)
            m_run = None


def _diff_attention(proj, lam_params, subln_g, *, batch, seq, heads, layer_idx, t):
    m = proj.shape[0]
    k_off = heads
    v_off = 2 * heads
    lambda_init = 0.8 - 0.6 * math.exp(-0.3 * layer_idx)
    acc_rows = HEAD_DIM + BF16_SUBLANES
    return pl.pallas_call(
        functools.partial(_diff_attn_kernel, t=t, lambda_init=lambda_init),
        grid=(batch, heads),
        in_specs=[pl.BlockSpec(memory_space=pltpu.SMEM),
                  pl.BlockSpec(lam_params.shape, lambda b, h: (0, 0)),
                  pl.BlockSpec((seq, HEAD_DIM), lambda b, h: (b, h)),
                  pl.BlockSpec((seq, HEAD_DIM), lambda b, h: (b, k_off + h)),
                  pl.BlockSpec((seq, HEAD_DIM), lambda b, h: (b, v_off + h)),
                  pl.BlockSpec((1, HEAD_DIM), lambda b, h: (0, 0))],
        out_specs=pl.BlockSpec((seq, HEAD_DIM), lambda b, h: (b, h)),
        out_shape=jax.ShapeDtypeStruct((m, heads * HEAD_DIM), BF16),
        scratch_shapes=([pltpu.VMEM((acc_rows, seq), BF16)]
                        + [pltpu.VMEM((2, t, t), F32)] * SCORE_BUFFERS
                        + [pltpu.VMEM((2, 1, t), F32)] * SCORE_BUFFERS
                        + [pltpu.VMEM((2, t, t), BF16),
                           pltpu.VMEM((acc_rows, t), F32), pltpu.VMEM((acc_rows, t), F32)]),
        compiler_params=pltpu.CompilerParams(
            dimension_semantics=("arbitrary", "arbitrary"),
            vmem_limit_bytes=VMEM_LIMIT_BYTES),
        name="diff_attention",
    )(jnp.zeros((1,), jnp.int32), lam_params, proj, proj, proj, subln_g.reshape(1, HEAD_DIM))


def _silu(g):
    return g * (1.0 / (1.0 + jnp.exp(-g)))


def _mix_out_kernel(*refs, conv, tm, tok_width, tiles_per_seq, mem_len):
    if conv:
        (xin_ref, gb_ref, gc_ref, cw_ref, qm_ref, gate_a_ref, gate_b_ref, kv_ref, wout_ref,
         x_ref, gpost_ref, o_ref, br_ref, y_ref, u_ref) = refs
    else:
        (tok_ref, qm_ref, gate_a_ref, gate_b_ref, kv_ref, wout_ref,
         x_ref, gpost_ref, o_ref, br_ref, y_ref) = refs
    half_gate = gate_a_ref.shape[1]
    chunk = MXU_DEPTH
    pad = CONV_PAD

    def gate_cols(rows, lo, hi):
        if hi <= half_gate:
            return _silu(gate_a_ref[rows, lo:hi])
        assert lo >= half_gate
        return _silu(gate_b_ref[rows, lo - half_gate:hi - half_gate])

    if conv:
        first = (pl.program_id(0) % tiles_per_seq) == 0

        @pl.when(first)
        def _():
            u_ref[0:pad, :] = jnp.zeros((pad, tok_width), F32)

        @pl.when(jnp.logical_not(first))
        def _():
            u_ref[0:pad, :] = u_ref[tm:tm + pad, :]

        u_ref[pad:pad + tm, :] = gc_ref[...].astype(F32) * xin_ref[...].astype(F32)

    def rows_part(r0, r1):
        rows = slice(r0, r1)

        def project(lo, hi, first_chunk):
            part = jnp.dot(br_ref[rows, lo:hi], wout_ref[lo:hi, :], preferred_element_type=F32)
            if first_chunk:
                y_ref[rows, :] = part
            else:
                y_ref[rows, :] += part

        for lo in range(0, tok_width, chunk):
            hi = lo + chunk
            if conv:
                w = cw_ref[:, lo:hi]
                conv_out = (u_ref[pad + r0:pad + r1, lo:hi] * w[2:3]
                            + u_ref[pad - 1 + r0:pad - 1 + r1, lo:hi] * w[1:2]
                            + u_ref[pad - 2 + r0:pad - 2 + r1, lo:hi] * w[0:1])
                tok = (gb_ref[rows, lo:hi].astype(F32) * conv_out).astype(BF16)
            else:
                tok = tok_ref[rows, lo:hi]
            br_ref[rows, lo:hi] = tok * gate_cols(rows, lo, hi)
            project(lo, hi, lo == 0)

        scale = HEAD_DIM ** -0.5
        for h in range(CROSS_HEADS):
            lo = h * HEAD_DIM
            q_h = qm_ref[rows, lo:lo + HEAD_DIM]
            k_h = kv_ref[:, lo:lo + HEAD_DIM]
            v_h = kv_ref[:, CROSS_WIDTH + lo:CROSS_WIDTH + lo + HEAD_DIM]
            s = lax.dot_general(q_h, k_h, (((1,), (1,)), ((), ())),
                                preferred_element_type=F32) * scale
            p = jnp.exp(s - jnp.max(s, axis=-1, keepdims=True))
            o_h = jnp.dot(p.astype(BF16), v_h, preferred_element_type=F32)
            o_h = o_h / jnp.sum(p, axis=-1, keepdims=True)
            col = tok_width + lo
            br_ref[rows, col:col + HEAD_DIM] = (
                o_h.astype(BF16) * gate_cols(rows, col, col + HEAD_DIM))
        for lo in range(tok_width, tok_width + CROSS_WIDTH, chunk):
            project(lo, lo + chunk, False)

        o_ref[rows, :] = x_ref[rows, :] + _rms_scale(y_ref[rows, :]) * gpost_ref[...]

    part_rows = tm // ROW_PARTS
    for r0 in range(0, tm, part_rows):
        rows_part(r0, r0 + part_rows)


def _mix_out(proj, tok, conv_w, kv, w_out_all_bf16, layer, x2d, g_post_all, *, seq, mem_len, tm):
    m, d = x2d.shape
    mix_width = w_out_all_bf16.shape[1]
    tok_width = mix_width - CROSS_WIDTH
    conv = tok is None
    tiles_per_seq = seq // tm
    qm_blk = (3 * tok_width) // CROSS_WIDTH
    half_gate = mix_width // 2
    gate_blk = (3 * tok_width + CROSS_WIDTH) // half_gate
    assert qm_blk * CROSS_WIDTH == 3 * tok_width
    assert gate_blk * half_gate == 3 * tok_width + CROSS_WIDTH

    tail_specs = [pl.BlockSpec((tm, CROSS_WIDTH), lambda i: (i, qm_blk)),
                  pl.BlockSpec((tm, half_gate), lambda i: (i, gate_blk)),
                  pl.BlockSpec((tm, half_gate), lambda i: (i, gate_blk + 1)),
                  pl.BlockSpec((mem_len, 2 * CROSS_WIDTH), lambda i: (i // tiles_per_seq, 0)),
                  pl.BlockSpec((None, mix_width, d), lambda i: (layer, 0, 0),
                               pipeline_mode=pl.Buffered(1)),
                  pl.BlockSpec((tm, d), lambda i: (i, 0)),
                  pl.BlockSpec((None, 1, d), lambda i: (layer, 0, 0))]
    tail_args = [proj, proj, proj, kv, w_out_all_bf16, x2d, g_post_all]
    scratch = [pltpu.VMEM((tm, mix_width), BF16), pltpu.VMEM((tm, d), F32)]
    if conv:
        head_specs = [pl.BlockSpec((tm, tok_width), lambda i: (i, 0)),
                      pl.BlockSpec((tm, tok_width), lambda i: (i, 1)),
                      pl.BlockSpec((tm, tok_width), lambda i: (i, 2)),
                      pl.BlockSpec((CONV_WIDTH, tok_width), lambda i: (0, 0))]
        head_args = [proj, proj, proj, conv_w.T]
        scratch.append(pltpu.VMEM((tm + CONV_PAD, tok_width), F32))
    else:
        head_specs = [pl.BlockSpec((tm, tok_width), lambda i: (i, 0))]
        head_args = [tok]
    return pl.pallas_call(
        functools.partial(_mix_out_kernel, conv=conv, tm=tm, tok_width=tok_width,
                          tiles_per_seq=tiles_per_seq, mem_len=mem_len),
        grid=(m // tm,),
        in_specs=head_specs + tail_specs,
        out_specs=pl.BlockSpec((tm, d), lambda i: (i, 0)),
        out_shape=jax.ShapeDtypeStruct((m, d), F32),
        scratch_shapes=scratch,
        compiler_params=pltpu.CompilerParams(
            dimension_semantics=("arbitrary",), vmem_limit_bytes=VMEM_LIMIT_BYTES),
        name="mix_out_conv" if conv else "mix_out_attn",
    )(*head_args, *tail_args)


def kernel(x, mem, positions, pre_norm, post_norm, mem_norm, w_in, w_kv_mem, w_out,
           conv_w, diff_lambda, diff_subln):
    batch, seq, d = x.shape
    mem_len = mem.shape[1]
    depth = w_in.shape[0]
    tok_width = w_out.shape[1] - CROSS_WIDTH
    heads = tok_width // HEAD_DIM
    x2d = x.reshape(batch * seq, d)
    mem2d = mem.reshape(batch * mem_len, d)
    rope = _rope_tables(positions, tm=ROPE_TABLE_ROWS) if depth > 1 else None
    w_kv_bf16 = w_kv_mem.astype(BF16)
    w_out_bf16 = w_out.astype(BF16)
    pre_g = pre_norm.reshape(depth, 1, d)
    post_g = post_norm.reshape(depth, 1, d)
    mem_g = mem_norm.reshape(depth, 1, d)

    for i in range(depth):
        attn_layer = (i % N_MIXERS) == 1
        proj = _in_proj(x2d, pre_g, w_in, i, rope if attn_layer else None,
                        tm=IN_PROJ_ROWS, tn=IN_PROJ_COLS, rope_cols=2 * tok_width)
        kv = _mem_kv(mem2d, mem_g, w_kv_bf16, i, mem_len=mem_len)
        if attn_layer:
            tok = _diff_attention(proj, diff_lambda[i // N_MIXERS], diff_subln[i // N_MIXERS],
                                  batch=batch, seq=seq, heads=heads, layer_idx=i, t=ATTN_TILE)
            conv = None
        else:
            tok = None
            conv = conv_w[i // N_MIXERS]
        x2d = _mix_out(proj, tok, conv, kv, w_out_bf16, i, x2d, post_g,
                       seq=seq, mem_len=mem_len, tm=MIX_ROWS)
    return x2d.reshape(batch, seq, d)
```

```python
import functools
import math

import jax
import jax.numpy as jnp
from jax import lax
from jax.experimental import pallas as pl
from jax.experimental.pallas import tpu as pltpu

HEAD_DIM = 128
CROSS_HEADS = 4
CROSS_WIDTH = CROSS_HEADS * HEAD_DIM
CONV_WIDTH = 3
DIFF_HEAD_DIM = 64
ROPE_THETA = 10000.0
RMS_EPS = 1e-6
NEG_BIG = -1e30
N_MIXERS = 2
LOG2_E = math.log2(math.e)

LANES = 128
BF16_SUBLANES = 16
CONV_PAD = 8
MXU_DEPTH = 256
ROW_PARTS = 2
SCORE_BUFFERS = 2
VMEM_LIMIT_BYTES = 56 * 1024 * 1024

IN_PROJ_ROWS = 512
IN_PROJ_VMEM_LIMIT_BYTES = 60 * 1024 * 1024
W_STAGE_COLS = 256
IN_PROJ_COLS = 1024
ROPE_TABLE_ROWS = 1024
ATTN_TILE = 512
MIX_ROWS = 512

F32 = jnp.float32
BF16 = jnp.bfloat16


def _rms_scale(xf):
    return xf * lax.rsqrt(jnp.mean(xf * xf, axis=-1, keepdims=True) + RMS_EPS)


def _rope_table_kernel(pos_ref, inv_ref, sign_ref, cos_ref, sin_ref):
    rows = pos_ref.shape[0]
    half = DIFF_HEAD_DIM // 2
    groups = LANES // half
    ang = pos_ref[...].astype(F32) * inv_ref[...]
    group_of_lane = lax.broadcasted_iota(jnp.int32, ang.shape, 1) // half
    for table, sign, out_ref in ((jnp.cos(ang), None, cos_ref),
                                 (jnp.sin(ang), sign_ref[...], sin_ref)):
        shifted = [table] + [pltpu.roll(table, d * half, 1) for d in range(1, groups)]
        for k in range(groups):
            g = shifted[(groups - k) % groups]
            for j in range(1, groups):
                g = jnp.where(group_of_lane == j, shifted[(j - k) % groups], g)
            out_ref[k * rows:(k + 1) * rows, :] = g if sign is None else g * sign


def _rope_tables(positions, tm):
    m = positions.size
    half = DIFF_HEAD_DIM // 2
    groups = LANES // half
    rows = tm // groups
    inv_freq = ROPE_THETA ** (-jnp.arange(0, DIFF_HEAD_DIM, 2, dtype=F32) / DIFF_HEAD_DIM)
    inv = jnp.tile(inv_freq, groups).reshape(1, LANES)
    sign = jnp.tile(jnp.concatenate([-jnp.ones((half,), F32), jnp.ones((half,), F32)]),
                    LANES // DIFF_HEAD_DIM).reshape(1, LANES)
    pos = positions.reshape(m // tm, groups, rows).transpose(0, 2, 1)
    pos = jnp.repeat(pos, half, axis=2).reshape(m // groups, LANES)
    return pl.pallas_call(
        _rope_table_kernel,
        grid=(m // tm,),
        in_specs=[pl.BlockSpec((rows, LANES), lambda i: (i, 0)),
                  pl.BlockSpec((1, LANES), lambda i: (0, 0)),
                  pl.BlockSpec((1, LANES), lambda i: (0, 0))],
        out_specs=[pl.BlockSpec((tm, LANES), lambda i: (i, 0)),
                   pl.BlockSpec((tm, LANES), lambda i: (i, 0))],
        out_shape=[jax.ShapeDtypeStruct((m, LANES), F32),
                   jax.ShapeDtypeStruct((m, LANES), F32)],
        name="rope_tables",
    )(pos, inv, sign)


def _in_proj_kernel(*refs, rope_tiles, tn, layer):
    if rope_tiles:
        x_ref, g_ref, w_hbm, cos_ref, sin_ref, o_ref, h_ref, w_ref, stage_ref, sem = refs
    else:
        x_ref, g_ref, w_hbm, o_ref, h_ref, w_ref, stage_ref, sem = refs

    def chunk_copy(c, slot):
        return pltpu.make_async_copy(
            w_hbm.at[layer, :, pl.ds(c * W_STAGE_COLS, W_STAGE_COLS)],
            stage_ref.at[slot], sem.at[slot])

    @pl.when(pl.program_id(0) == 0)
    def _():
        n_chunks = w_ref.shape[1] // W_STAGE_COLS
        chunk_copy(0, 0).start(priority=0)
        for c in range(n_chunks):
            slot = c % 2
            if c + 1 < n_chunks:
                chunk_copy(c + 1, 1 - slot).start(priority=1 - slot)
            chunk_copy(c, slot).wait()
            w_ref[:, c * W_STAGE_COLS:(c + 1) * W_STAGE_COLS] = stage_ref[slot].astype(BF16)

    def emit(c, acc, rotary):
        base = c * tn
        if not rotary:
            o_ref[:, base:base + tn] = acc.astype(o_ref.dtype)
            return
        cos = cos_ref[...]
        sin = sin_ref[...]
        lane = lax.broadcasted_iota(jnp.int32, cos.shape, 1)
        first_half = (lane % DIFF_HEAD_DIM) < (DIFF_HEAD_DIM // 2)
        half = DIFF_HEAD_DIM // 2
        for k in range(tn // LANES):
            t = acc[:, k * LANES:(k + 1) * LANES]
            partner = jnp.where(first_half, pltpu.roll(t, LANES - half, 1), pltpu.roll(t, half, 1))
            o_ref[:, base + k * LANES:base + (k + 1) * LANES] = (
                t * cos + partner * sin).astype(o_ref.dtype)

    h_ref[...] = (_rms_scale(x_ref[...]) * g_ref[...]).astype(BF16)
    for c in range(w_ref.shape[1] // tn):
        acc = jnp.dot(h_ref[...], w_ref[:, c * tn:(c + 1) * tn], preferred_element_type=F32)
        emit(c, acc, c < rope_tiles)


def _in_proj(x2d, g_all, w_all, layer, rope, *, tm, tn, rope_cols):
    m, d = x2d.shape
    n = w_all.shape[2]
    assert n % W_STAGE_COLS == 0
    rope_tiles = 0 if rope is None else rope_cols // tn
    in_specs = [pl.BlockSpec((tm, d), lambda i: (i, 0)),
                pl.BlockSpec((None, 1, d), lambda i: (layer, 0, 0)),
                pl.BlockSpec(memory_space=pl.ANY)]
    args = [x2d, g_all, w_all]
    if rope is not None:
        assert rope_cols % tn == 0
        in_specs += [pl.BlockSpec((tm, LANES), lambda i: (i, 0)),
                     pl.BlockSpec((tm, LANES), lambda i: (i, 0))]
        args += list(rope)
    return pl.pallas_call(
        functools.partial(_in_proj_kernel, rope_tiles=rope_tiles, tn=tn, layer=layer),
        grid=(m // tm,),
        in_specs=in_specs,
        out_specs=pl.BlockSpec((tm, n), lambda i: (i, 0)),
        out_shape=jax.ShapeDtypeStruct((m, n), BF16),
        scratch_shapes=[pltpu.VMEM((tm, d), BF16), pltpu.VMEM((d, n), BF16),
                        pltpu.VMEM((2, d, W_STAGE_COLS), F32), pltpu.SemaphoreType.DMA((2,))],
        compiler_params=pltpu.CompilerParams(
            dimension_semantics=("arbitrary",),
            vmem_limit_bytes=IN_PROJ_VMEM_LIMIT_BYTES),
        name="in_proj_rope" if rope_tiles else "in_proj",
    )(*args)


def _mem_kv_kernel(mem_ref, g_ref, w_ref, o_ref):
    h = (_rms_scale(mem_ref[...]) * g_ref[...]).astype(BF16)
    o_ref[...] = jnp.dot(h, w_ref[...], preferred_element_type=F32).astype(o_ref.dtype)


def _mem_kv(mem2d, g_all, w_all_bf16, layer, *, mem_len):
    m, d = mem2d.shape
    n = w_all_bf16.shape[2]
    return pl.pallas_call(
        _mem_kv_kernel,
        grid=(m // mem_len,),
        in_specs=[pl.BlockSpec((mem_len, d), lambda i: (i, 0)),
                  pl.BlockSpec((None, 1, d), lambda i: (layer, 0, 0)),
                  pl.BlockSpec((None, d, n), lambda i: (layer, 0, 0))],
        out_specs=pl.BlockSpec((mem_len, n), lambda i: (i, 0)),
        out_shape=jax.ShapeDtypeStruct((m, n), BF16),
        compiler_params=pltpu.CompilerParams(
            dimension_semantics=("arbitrary",), vmem_limit_bytes=VMEM_LIMIT_BYTES),
        name="mem_kv",
    )(mem2d, g_all, w_all_bf16)


def _diff_attn_kernel(zero_ref, lam_ref, q_ref, k_ref, v_ref, g_ref, o_ref, vt_ref, *scratch,
                      t, lambda_init):
    s_refs = scratch[:SCORE_BUFFERS]
    mx_refs = scratch[SCORE_BUFFERS:2 * SCORE_BUFFERS]
    p_ref, acc1_ref, acc2_ref = scratch[2 * SCORE_BUFFERS:]
    nq = q_ref.shape[0] // t
    n_maps = 2
    acc_refs = (acc1_ref, acc2_ref)
    bufs = tuple(zip(s_refs, mx_refs))
    z = zero_ref[0]
    half = t // 2

    ones = jnp.ones((BF16_SUBLANES, t), BF16)
    for j in range(nq):
        vt_ref[0:HEAD_DIM, j * t:(j + 1) * t] = (
            v_ref[j * t:(j + 1) * t, :].astype(F32).T.astype(BF16))
        vt_ref[HEAD_DIM:, j * t:(j + 1) * t] = ones

    lp = lam_ref[...]
    lam = (jnp.exp(jnp.sum(lp[0:1] * lp[1:2], axis=1, keepdims=True))
           - jnp.exp(jnp.sum(lp[2:3] * lp[3:4], axis=1, keepdims=True)) + lambda_init)

    def visible(n_keys, n_queries):
        key = lax.broadcasted_iota(jnp.int32, (n_keys, n_queries), 0)
        qry = lax.broadcasted_iota(jnp.int32, (n_keys, n_queries), 1)
        return key <= qry

    def q_maps_of(qi):
        q_t = (q_ref[qi * t:(qi + 1) * t, :].astype(F32) * (LOG2_E * DIFF_HEAD_DIM ** -0.5)).T
        row = lax.broadcasted_iota(jnp.int32, q_t.shape, 0)
        zero = jnp.zeros_like(q_t)
        return (jnp.where(row < DIFF_HEAD_DIM, q_t, zero).astype(BF16),
                jnp.where(row >= DIFF_HEAD_DIM, q_t, zero).astype(BF16))

    def scores(q_maps, j, diagonal, buf):
        s_ref, mx_ref = buf
        for mi in range(n_maps):
            if not diagonal:
                s = jnp.dot(k_ref[j * t:(j + 1) * t, :], q_maps[mi],
                            preferred_element_type=F32)
                s_ref[z + mi] = s
                mx_ref[mi] = jnp.max(s, axis=0, keepdims=True)
                continue
            s_top = jnp.dot(k_ref[j * t:j * t + half, :], q_maps[mi],
                            preferred_element_type=F32)
            s_top = jnp.where(visible(half, t), s_top, NEG_BIG)
            s_bot = jnp.dot(k_ref[j * t + half:(j + 1) * t, :], q_maps[mi][:, half:],
                            preferred_element_type=F32)
            s_bot = jnp.where(visible(half, half), s_bot, NEG_BIG)
            s_ref[z + mi, 0:half, :] = s_top
            s_ref[z + mi, half:, half:] = s_bot
            mx_top = jnp.max(s_top, axis=0, keepdims=True)
            mx_ref[mi, :, 0:half] = mx_top[:, :half]
            mx_ref[mi, :, half:] = jnp.maximum(mx_top[:, half:],
                                               jnp.max(s_bot, axis=0, keepdims=True))

    def consume(j, diagonal, buf, m_old):
        s_ref, mx_ref = buf
        v_t = vt_ref[:, j * t:(j + 1) * t]
        m_out = []
        for mi in range(n_maps):
            acc_ref = acc_refs[mi]
            if m_old is None:
                m_new = mx_ref[mi]
                alpha = None
            else:
                m_new = jnp.maximum(m_old[mi], mx_ref[mi])
                alpha = jnp.exp2(m_old[mi] - m_new)
            if not diagonal:
                p_ref[z + mi] = jnp.exp2(s_ref[z + mi] - m_new).astype(BF16)
                pv = jnp.dot(v_t, p_ref[z + mi], preferred_element_type=F32)
                acc_ref[...] = pv if alpha is None else alpha * acc_ref[...] + pv
            else:
                p_ref[z + mi, 0:half, :] = jnp.exp2(
                    s_ref[z + mi, 0:half, :] - m_new).astype(BF16)
                p_ref[z + mi, half:, half:] = jnp.exp2(
                    s_ref[z + mi, half:, half:] - m_new[:, half:]).astype(BF16)
                pv = jnp.dot(v_t[:, :half], p_ref[z + mi, 0:half, :],
                             preferred_element_type=F32)
                pv_late = jnp.dot(v_t[:, half:], p_ref[z + mi, half:, half:],
                                  preferred_element_type=F32)
                if alpha is None:
                    acc_ref[:, :half] = pv[:, :half]
                    acc_ref[:, half:] = pv[:, half:] + pv_late
                else:
                    acc_ref[:, :half] = alpha[:, :half] * acc_ref[:, :half] + pv[:, :half]
                    acc_ref[:, half:] = (alpha[:, half:] * acc_ref[:, half:]
                                         + pv[:, half:] + pv_late)
            m_out.append(m_new)
        return m_out

    def finalize(qi):
        o1 = acc1_ref[0:HEAD_DIM, :] / acc1_ref[HEAD_DIM:HEAD_DIM + 1, :]
        o2 = acc2_ref[0:HEAD_DIM, :] / acc2_ref[HEAD_DIM:HEAD_DIM + 1, :]
        o = o1 - lam * o2
        o = o * lax.rsqrt(jnp.mean(o * o, axis=0, keepdims=True) + RMS_EPS)
        o_ref[qi * t:(qi + 1) * t, :] = (
            (o.T * g_ref[...]) * (1.0 - lambda_init)).astype(o_ref.dtype)

    tiles = [(qi, j) for qi in range(nq) for j in range(qi + 1)]
    ahead = SCORE_BUFFERS - 1
    q_maps_row = {}

    def issue_scores(i):
        qn, jn = tiles[i]
        if qn not in q_maps_row:
            q_maps_row[qn] = q_maps_of(qn)
        scores(q_maps_row[qn], jn, jn == qn, bufs[i % SCORE_BUFFERS])

    for i in range(min(ahead, len(tiles))):
        issue_scores(i)
    m_run = None
    for i, (qi, j) in enumerate(tiles):
        if i + ahead < len(tiles):
            issue_scores(i + ahead)
        m_run = consume(j, j == qi, bufs[i % SCORE_BUFFERS], m_run)
        if j == qi:
            finalize(qi)
            m_run = None


def _diff_attention(proj, lam_params, subln_g, *, batch, seq, heads, layer_idx, t):
    m = proj.shape[0]
    k_off = heads
    v_off = 2 * heads
    lambda_init = 0.8 - 0.6 * math.exp(-0.3 * layer_idx)
    acc_rows = HEAD_DIM + BF16_SUBLANES
    return pl.pallas_call(
        functools.partial(_diff_attn_kernel, t=t, lambda_init=lambda_init),
        grid=(batch, heads),
        in_specs=[pl.BlockSpec(memory_space=pltpu.SMEM),
                  pl.BlockSpec(lam_params.shape, lambda b, h: (0, 0)),
                  pl.BlockSpec((seq, HEAD_DIM), lambda b, h: (b, h)),
                  pl.BlockSpec((seq, HEAD_DIM), lambda b, h: (b, k_off + h)),
                  pl.BlockSpec((seq, HEAD_DIM), lambda b, h: (b, v_off + h)),
                  pl.BlockSpec((1, HEAD_DIM), lambda b, h: (0, 0))],
        out_specs=pl.BlockSpec((seq, HEAD_DIM), lambda b, h: (b, h)),
        out_shape=jax.ShapeDtypeStruct((m, heads * HEAD_DIM), BF16),
        scratch_shapes=([pltpu.VMEM((acc_rows, seq), BF16)]
                        + [pltpu.VMEM((2, t, t), F32)] * SCORE_BUFFERS
                        + [pltpu.VMEM((2, 1, t), F32)] * SCORE_BUFFERS
                        + [pltpu.VMEM((2, t, t), BF16),
                           pltpu.VMEM((acc_rows, t), F32), pltpu.VMEM((acc_rows, t), F32)]),
        compiler_params=pltpu.CompilerParams(
            dimension_semantics=("arbitrary", "arbitrary"),
            vmem_limit_bytes=VMEM_LIMIT_BYTES),
        name="diff_attention",
    )(jnp.zeros((1,), jnp.int32), lam_params, proj, proj, proj, subln_g.reshape(1, HEAD_DIM))


def _silu(g):
    return g * (1.0 / (1.0 + jnp.exp(-g)))


def _mix_out_kernel(*refs, conv, tm, tok_width, tiles_per_seq, mem_len):
    if conv:
        (xin_ref, gb_ref, gc_ref, cw_ref, qm_ref, gate_a_ref, gate_b_ref, kv_ref, wout_ref,
         x_ref, gpost_ref, o_ref, br_ref, y_ref, u_ref) = refs
    else:
        (tok_ref, qm_ref, gate_a_ref, gate_b_ref, kv_ref, wout_ref,
         x_ref, gpost_ref, o_ref, br_ref, y_ref) = refs
    half_gate = gate_a_ref.shape[1]
    chunk = MXU_DEPTH
    pad = CONV_PAD

    def gate_cols(rows, lo, hi):
        if hi <= half_gate:
            return _silu(gate_a_ref[rows, lo:hi])
        assert lo >= half_gate
        return _silu(gate_b_ref[rows, lo - half_gate:hi - half_gate])

    if conv:
        first = (pl.program_id(0) % tiles_per_seq) == 0

        @pl.when(first)
        def _():
            u_ref[0:pad, :] = jnp.zeros((pad, tok_width), F32)

        @pl.when(jnp.logical_not(first))
        def _():
            u_ref[0:pad, :] = u_ref[tm:tm + pad, :]

        u_ref[pad:pad + tm, :] = gc_ref[...].astype(F32) * xin_ref[...].astype(F32)

    def rows_part(r0, r1):
        rows = slice(r0, r1)

        def project(lo, hi, first_chunk):
            part = jnp.dot(br_ref[rows, lo:hi], wout_ref[lo:hi, :], preferred_element_type=F32)
            if first_chunk:
                y_ref[rows, :] = part
            else:
                y_ref[rows, :] += part

        for lo in range(0, tok_width, chunk):
            hi = lo + chunk
            if conv:
                w = cw_ref[:, lo:hi]
                conv_out = (u_ref[pad + r0:pad + r1, lo:hi] * w[2:3]
                            + u_ref[pad - 1 + r0:pad - 1 + r1, lo:hi] * w[1:2]
                            + u_ref[pad - 2 + r0:pad - 2 + r1, lo:hi] * w[0:1])
                tok = (gb_ref[rows, lo:hi].astype(F32) * conv_out).astype(BF16)
            else:
                tok = tok_ref[rows, lo:hi]
            br_ref[rows, lo:hi] = tok * gate_cols(rows, lo, hi)
            project(lo, hi, lo == 0)

        scale = HEAD_DIM ** -0.5
        for h in range(CROSS_HEADS):
            lo = h * HEAD_DIM
            q_h = qm_ref[rows, lo:lo + HEAD_DIM]
            k_h = kv_ref[:, lo:lo + HEAD_DIM]
            v_h = kv_ref[:, CROSS_WIDTH + lo:CROSS_WIDTH + lo + HEAD_DIM]
            s = lax.dot_general(q_h, k_h, (((1,), (1,)), ((), ())),
                                preferred_element_type=F32) * scale
            p = jnp.exp(s - jnp.max(s, axis=-1, keepdims=True))
            o_h = jnp.dot(p.astype(BF16), v_h, preferred_element_type=F32)
            o_h = o_h / jnp.sum(p, axis=-1, keepdims=True)
            col = tok_width + lo
            br_ref[rows, col:col + HEAD_DIM] = (
                o_h.astype(BF16) * gate_cols(rows, col, col + HEAD_DIM))
        for lo in range(tok_width, tok_width + CROSS_WIDTH, chunk):
            project(lo, lo + chunk, False)

        o_ref[rows, :] = x_ref[rows, :] + _rms_scale(y_ref[rows, :]) * gpost_ref[...]

    part_rows = tm // ROW_PARTS
    for r0 in range(0, tm, part_rows):
        rows_part(r0, r0 + part_rows)


def _mix_out(proj, tok, conv_w, kv, w_out_all_bf16, layer, x2d, g_post_all, *, seq, mem_len, tm):
    m, d = x2d.shape
    mix_width = w_out_all_bf16.shape[1]
    tok_width = mix_width - CROSS_WIDTH
    conv = tok is None
    tiles_per_seq = seq // tm
    qm_blk = (3 * tok_width) // CROSS_WIDTH
    half_gate = mix_width // 2
    gate_blk = (3 * tok_width + CROSS_WIDTH) // half_gate
    assert qm_blk * CROSS_WIDTH == 3 * tok_width
    assert gate_blk * half_gate == 3 * tok_width + CROSS_WIDTH

    tail_specs = [pl.BlockSpec((tm, CROSS_WIDTH), lambda i: (i, qm_blk)),
                  pl.BlockSpec((tm, half_gate), lambda i: (i, gate_blk)),
                  pl.BlockSpec((tm, half_gate), lambda i: (i, gate_blk + 1)),
                  pl.BlockSpec((mem_len, 2 * CROSS_WIDTH), lambda i: (i // tiles_per_seq, 0)),
                  pl.BlockSpec((None, mix_width, d), lambda i: (layer, 0, 0),
                               pipeline_mode=pl.Buffered(1)),
                  pl.BlockSpec((tm, d), lambda i: (i, 0)),
                  pl.BlockSpec((None, 1, d), lambda i: (layer, 0, 0))]
    tail_args = [proj, proj, proj, kv, w_out_all_bf16, x2d, g_post_all]
    scratch = [pltpu.VMEM((tm, mix_width), BF16), pltpu.VMEM((tm, d), F32)]
    if conv:
        head_specs = [pl.BlockSpec((tm, tok_width), lambda i: (i, 0)),
                      pl.BlockSpec((tm, tok_width), lambda i: (i, 1)),
                      pl.BlockSpec((tm, tok_width), lambda i: (i, 2)),
                      pl.BlockSpec((CONV_WIDTH, tok_width), lambda i: (0, 0))]
        head_args = [proj, proj, proj, conv_w.T]
        scratch.append(pltpu.VMEM((tm + CONV_PAD, tok_width), F32))
    else:
        head_specs = [pl.BlockSpec((tm, tok_width), lambda i: (i, 0))]
        head_args = [tok]
    return pl.pallas_call(
        functools.partial(_mix_out_kernel, conv=conv, tm=tm, tok_width=tok_width,
                          tiles_per_seq=tiles_per_seq, mem_len=mem_len),
        grid=(m // tm,),
        in_specs=head_specs + tail_specs,
        out_specs=pl.BlockSpec((tm, d), lambda i: (i, 0)),
        out_shape=jax.ShapeDtypeStruct((m, d), F32),
        scratch_shapes=scratch,
        compiler_params=pltpu.CompilerParams(
            dimension_semantics=("arbitrary",), vmem_limit_bytes=VMEM_LIMIT_BYTES),
        name="mix_out_conv" if conv else "mix_out_attn",
    )(*head_args, *tail_args)


def kernel(x, mem, positions, pre_norm, post_norm, mem_norm, w_in, w_kv_mem, w_out,
           conv_w, diff_lambda, diff_subln):
    batch, seq, d = x.shape
    mem_len = mem.shape[1]
    depth = w_in.shape[0]
    tok_width = w_out.shape[1] - CROSS_WIDTH
    heads = tok_width // HEAD_DIM
    x2d = x.reshape(batch * seq, d)
    mem2d = mem.reshape(batch * mem_len, d)
    rope = _rope_tables(positions, tm=ROPE_TABLE_ROWS) if depth > 1 else None
    w_kv_bf16 = w_kv_mem.astype(BF16)
    w_out_bf16 = w_out.astype(BF16)
    pre_g = pre_norm.reshape(depth, 1, d)
    post_g = post_norm.reshape(depth, 1, d)
    mem_g = mem_norm.reshape(depth, 1, d)

    for i in range(depth):
        attn_layer = (i % N_MIXERS) == 1
        proj = _in_proj(x2d, pre_g, w_in, i, rope if attn_layer else None,
                        tm=IN_PROJ_ROWS, tn=IN_PROJ_COLS, rope_cols=2 * tok_width)
        kv = _mem_kv(mem2d, mem_g, w_kv_bf16, i, mem_len=mem_len)
        if attn_layer:
            tok = _diff_attention(proj, diff_lambda[i // N_MIXERS], diff_subln[i // N_MIXERS],
                                  batch=batch, seq=seq, heads=heads, layer_idx=i, t=ATTN_TILE)
            conv = None
        else:
            tok = None
            conv = conv_w[i // N_MIXERS]
        x2d = _mix_out(proj, tok, conv, kv, w_out_bf16, i, x2d, post_g,
                       seq=seq, mem_len=mem_len, tm=MIX_ROWS)
    return x2d.reshape(batch, seq, d)
```
